```python
import jax, jax.numpy as jnp
from jax import lax
import numpy as np

D_MODEL = 2048
BATCH = 2
SEQ = 4096
DEPTH = 4
DEC_BATCH = 16
DEC_SEQ = 32
PAST_LEN = 4096

CHUNK = 64
CONV_DIM = 1024
CONV_WIDTH = 31
GMLP_DIM = 1024
GMLP_GROUPS = 8
GMLP_GC = GMLP_DIM // GMLP_GROUPS
GMLP_CHUNK = 128
N_HEADS = 8
HEAD_DIM = 128
ATTN_DIM = N_HEADS * HEAD_DIM
LEFT_CHUNKS = 8
ATTN_REACH = LEFT_CHUNKS * CHUNK
REL_CLIP = 128
D_FF = 4 * D_MODEL
PLE_DIM = 256
N_BRANCH = 3
EPS = 1e-6
IN_SPLITS = (2 * CONV_DIM,
             2 * CONV_DIM + 2 * GMLP_DIM,
             2 * CONV_DIM + 2 * GMLP_DIM + ATTN_DIM,
             2 * CONV_DIM + 2 * GMLP_DIM + 2 * ATTN_DIM,
             2 * CONV_DIM + 2 * GMLP_DIM + 3 * ATTN_DIM)
IN_COLS = 2 * CONV_DIM + 2 * GMLP_DIM + 3 * ATTN_DIM + N_BRANCH * D_MODEL

kernel_name = 'hybrid_streaming_encoder_step'


def rmsnorm(x, g):
    xf = x.astype(jnp.float32)
    y = xf * lax.rsqrt(jnp.mean(xf * xf, axis=-1, keepdims=True) + EPS)
    return (y * g.astype(jnp.float32)).astype(x.dtype)


def layernorm(x, g, b):
    xf = x.astype(jnp.float32)
    mu = jnp.mean(xf, axis=-1, keepdims=True)
    xc = xf - mu
    var = jnp.mean(xc * xc, axis=-1, keepdims=True)
    y = xc * lax.rsqrt(var + EPS) * g.astype(jnp.float32) + b.astype(jnp.float32)
    return y.astype(x.dtype)


def conv_module(za, conv_cache, conv_w, conv_b, ln_g, ln_b, w_a):
    a = za[..., :CONV_DIM] * jax.nn.sigmoid(za[..., CONV_DIM:])
    if conv_cache is None:
        left = jnp.zeros((a.shape[0], CONV_WIDTH - 1, CONV_DIM), a.dtype)
    else:
        left = conv_cache.astype(a.dtype)
    a_ext = jnp.concatenate([left, a], axis=1)
    y = lax.conv_general_dilated(a_ext, conv_w[:, None, :].astype(a.dtype), (1,), 'VALID',
                                 dimension_numbers=('NWC', 'WIO', 'NWC'),
                                 feature_group_count=CONV_DIM) + conv_b
    y = jax.nn.silu(layernorm(y, ln_g, ln_b))
    return y @ w_a, a_ext[:, -(CONV_WIDTH - 1):]


def gmlp_spatial(v, w_s, b_s):
    bsz, t = v.shape[0], v.shape[1]
    nb = -(-t // GMLP_CHUNK)
    tp = nb * GMLP_CHUNK
    vp = jnp.pad(v, ((0, 0), (0, tp - t), (0, 0), (0, 0)))
    vp = vp.reshape(bsz, nb, GMLP_CHUNK, GMLP_GROUPS, GMLP_GC)
    mask = jnp.tril(jnp.ones((GMLP_CHUNK, GMLP_CHUNK), dtype=bool))
    ws = jnp.where(mask[None], w_s, 0)
    out = jnp.einsum('gts,bnsgc->bntgc', ws, vp) + jnp.transpose(b_s)[None, None, :, :, None]
    return out.reshape(bsz, tp, GMLP_DIM)[:, :t]


def rel_bias(rel_tab, rel):
    idx = jnp.clip(rel, -REL_CLIP, REL_CLIP) + REL_CLIP
    return rel_tab[:, idx].astype(jnp.float32)


def band_attention_prompt(q, k, v, rel_tab):
    bsz, t = q.shape[0], q.shape[1]
    nc = t // CHUNK
    nband = (LEFT_CHUNKS + 1) * CHUNK
    pad = ((0, 0), (ATTN_REACH, 0), (0, 0), (0, 0))
    kp = jnp.pad(k, pad).reshape(bsz, nc + LEFT_CHUNKS, CHUNK, N_HEADS, HEAD_DIM)
    vp = jnp.pad(v, pad).reshape(bsz, nc + LEFT_CHUNKS, CHUNK, N_HEADS, HEAD_DIM)
    kb = jnp.concatenate([kp[:, j:j + nc] for j in range(LEFT_CHUNKS + 1)], axis=2)
    vb = jnp.concatenate([vp[:, j:j + nc] for j in range(LEFT_CHUNKS + 1)], axis=2)
    qc = q.reshape(bsz, nc, CHUNK, N_HEADS, HEAD_DIM)
    s = jnp.einsum('bnqhd,bnkhd->bnhqk', qc, kb).astype(jnp.float32) * (HEAD_DIM ** -0.5)
    k_off = jnp.arange(nband) - ATTN_REACH
    rel = jnp.arange(CHUNK)[:, None] - k_off[None, :]
    s = s + rel_bias(rel_tab, rel)[None, None]
    valid = (jnp.arange(nc)[:, None] * CHUNK + k_off[None, :]) >= 0
    s = jnp.where(valid[None, :, None, None, :], s, -1e30)
    p = jax.nn.softmax(s, axis=-1).astype(v.dtype)
    o = jnp.einsum('bnhqk,bnkhd->bnqhd', p, vb)
    return o.reshape(bsz, t, N_HEADS, HEAD_DIM)


def band_attention_sample(q, k, v, k_cache, v_cache, rel_tab):
    s_len = q.shape[1]
    w = k_cache.shape[1]
    kk = jnp.concatenate([k_cache.astype(k.dtype), k], axis=1)
    vv = jnp.concatenate([v_cache.astype(v.dtype), v], axis=1)
    s = jnp.einsum('bqhd,bkhd->bhqk', q, kk).astype(jnp.float32) * (HEAD_DIM ** -0.5)
    rel = jnp.arange(s_len)[:, None] + w - jnp.arange(w + s_len)[None, :]
    s = s + rel_bias(rel_tab, rel)[None]
    p = jax.nn.softmax(s, axis=-1).astype(v.dtype)
    return jnp.einsum('bhqk,bkhd->bqhd', p, vv)


def layer(h, ple, conv_cache, k_cache, v_cache, w):
    (g_mix, w_in, conv_w, conv_b, conv_ln_g, conv_ln_b, w_a,
     gm_ln_g, gm_ln_b, gm_ws, gm_bs, w_b, rel_tab, w_c, w_o,
     g_ffn, w_ff1, w_ff2, g_ple, w_ple_gate, w_ple_proj) = w
    bsz, t = h.shape[0], h.shape[1]
    xn = rmsnorm(h, g_mix)
    z = jnp.einsum('btd,dc->btc', xn, w_in)
    za, zb, zq, zk, zv, zg = jnp.split(z, IN_SPLITS, axis=-1)
    ya, conv_state = conv_module(za, conv_cache, conv_w, conv_b, conv_ln_g, conv_ln_b, w_a)
    u, v = jnp.split(jax.nn.gelu(zb), 2, axis=-1)
    v = layernorm(v, gm_ln_g, gm_ln_b)
    yb = (u * gmlp_spatial(v.reshape(bsz, t, GMLP_GROUPS, GMLP_GC), gm_ws, gm_bs)) @ w_b
    q = zq.reshape(bsz, t, N_HEADS, HEAD_DIM)
    k = zk.reshape(bsz, t, N_HEADS, HEAD_DIM)
    vv = zv.reshape(bsz, t, N_HEADS, HEAD_DIM)
    if k_cache is None:
        o = band_attention_prompt(q, k, vv, rel_tab)
        reach = min(ATTN_REACH, t)
        k_rows, v_rows = k[:, t - reach:], vv[:, t - reach:]
    else:
        o = band_attention_sample(q, k, vv, k_cache, v_cache, rel_tab)
        k_rows, v_rows = k, vv
    yc = o.reshape(bsz, t, ATTN_DIM) @ w_c
    gates = jax.nn.sigmoid(zg.reshape(bsz, t, N_BRANCH, D_MODEL))
    merged = gates[:, :, 0] * ya + gates[:, :, 1] * yb + gates[:, :, 2] * yc
    h = h + merged @ w_o
    hn = rmsnorm(h, g_ffn)
    h = h + jnp.square(jax.nn.relu(hn @ w_ff1)) @ w_ff2
    gate = jax.nn.sigmoid(rmsnorm(h, g_ple) @ w_ple_gate)
    h = h + gate * (ple @ w_ple_proj)
    return h, conv_state, k_rows, v_rows, v


def setup_inputs(seed: int = 0) -> dict:
    key = jax.random.key(seed)
    ks = jax.random.split(key, 32)

    def nrm(k, shape, scale):
        return jax.random.normal(k, shape, jnp.float32) * scale

    w_cache = min(ATTN_REACH, PAST_LEN)
    return {
        'x_prompt': nrm(ks[0], (BATCH, SEQ, D_MODEL), 1.0),
        'x_sample': nrm(ks[1], (DEC_BATCH, DEC_SEQ, D_MODEL), 1.0),
        'p_prompt': nrm(ks[2], (DEPTH, BATCH, SEQ, PLE_DIM), 1.0),
        'p_sample': nrm(ks[3], (DEPTH, DEC_BATCH, DEC_SEQ, PLE_DIM), 1.0),
        'cache_conv': nrm(ks[4], (DEPTH, DEC_BATCH, CONV_WIDTH - 1, CONV_DIM), 0.5),
        'cache_k': nrm(ks[5], (DEPTH, DEC_BATCH, w_cache, N_HEADS, HEAD_DIM), 1.0),
        'cache_v': nrm(ks[6], (DEPTH, DEC_BATCH, w_cache, N_HEADS, HEAD_DIM), 1.0),
        'norm_mix': 1.0 + nrm(ks[7], (DEPTH, D_MODEL), 0.02),
        'w_in': nrm(ks[8], (DEPTH, D_MODEL, IN_COLS), D_MODEL ** -0.5),
        'conv_w': nrm(ks[9], (DEPTH, CONV_WIDTH, CONV_DIM), CONV_WIDTH ** -0.5),
        'conv_b': nrm(ks[10], (DEPTH, CONV_DIM), 0.02),
        'conv_ln_g': 1.0 + nrm(ks[11], (DEPTH, CONV_DIM), 0.02),
        'conv_ln_b': nrm(ks[12], (DEPTH, CONV_DIM), 0.02),
        'w_a_out': nrm(ks[13], (DEPTH, CONV_DIM, D_MODEL), CONV_DIM ** -0.5),
        'gmlp_ln_g': 1.0 + nrm(ks[14], (DEPTH, GMLP_DIM), 0.02),
        'gmlp_ln_b': nrm(ks[15], (DEPTH, GMLP_DIM), 0.02),
        'gmlp_ws': nrm(ks[16], (DEPTH, GMLP_GROUPS, GMLP_CHUNK, GMLP_CHUNK), 0.5 * GMLP_CHUNK ** -0.5),
        'gmlp_bs': 1.0 + nrm(ks[17], (DEPTH, GMLP_GROUPS, GMLP_CHUNK), 0.02),
        'w_b_out': nrm(ks[18], (DEPTH, GMLP_DIM, D_MODEL), GMLP_DIM ** -0.5),
        'attn_rel_bias': nrm(ks[19], (DEPTH, N_HEADS, 2 * REL_CLIP + 1), 0.5),
        'w_c_out': nrm(ks[20], (DEPTH, ATTN_DIM, D_MODEL), ATTN_DIM ** -0.5),
        'w_o': nrm(ks[21], (DEPTH, D_MODEL, D_MODEL), D_MODEL ** -0.5),
        'norm_ffn': 1.0 + nrm(ks[22], (DEPTH, D_MODEL), 0.02),
        'w_ff1': nrm(ks[23], (DEPTH, D_MODEL, D_FF), D_MODEL ** -0.5),
        'w_ff2': nrm(ks[24], (DEPTH, D_FF, D_MODEL), D_FF ** -0.5),
        'norm_ple': 1.0 + nrm(ks[25], (DEPTH, D_MODEL), 0.02),
        'w_ple_gate': nrm(ks[26], (DEPTH, D_MODEL, D_MODEL), D_MODEL ** -0.5),
        'w_ple_proj': nrm(ks[27], (DEPTH, PLE_DIM, D_MODEL), PLE_DIM ** -0.5),
        'norm_final': 1.0 + nrm(ks[28], (D_MODEL,), 0.02),
    }


def reference(x_prompt, x_sample, p_prompt, p_sample, cache_conv, cache_k, cache_v,
              norm_mix, w_in, conv_w, conv_b, conv_ln_g, conv_ln_b, w_a_out,
              gmlp_ln_g, gmlp_ln_b, gmlp_ws, gmlp_bs, w_b_out, attn_rel_bias, w_c_out, w_o,
              norm_ffn, w_ff1, w_ff2, norm_ple, w_ple_gate, w_ple_proj, norm_final):
    h_p, h_s = x_prompt, x_sample
    conv_p, conv_s, kp_rows, vp_rows, ks_rows, vs_rows, gv_s = [], [], [], [], [], [], []
    for i in range(DEPTH):
        w = (norm_mix[i], w_in[i], conv_w[i], conv_b[i], conv_ln_g[i], conv_ln_b[i], w_a_out[i],
             gmlp_ln_g[i], gmlp_ln_b[i], gmlp_ws[i], gmlp_bs[i], w_b_out[i], attn_rel_bias[i],
             w_c_out[i], w_o[i], norm_ffn[i], w_ff1[i], w_ff2[i], norm_ple[i], w_ple_gate[i],
             w_ple_proj[i])
        h_p, c_p, k_p, v_p, _ = layer(h_p, p_prompt[i], None, None, None, w)
        h_s, c_s, k_s, v_s, g_s = layer(h_s, p_sample[i], cache_conv[i], cache_k[i], cache_v[i], w)
        conv_p.append(c_p)
        conv_s.append(c_s)
        kp_rows.append(k_p)
        vp_rows.append(v_p)
        ks_rows.append(k_s)
        vs_rows.append(v_s)
        gv_s.append(g_s)
    y_prompt = rmsnorm(h_p, norm_final)
    y_sample = rmsnorm(h_s, norm_final)
    conv_state_prompt = jnp.stack(conv_p)
    conv_state_sample = jnp.stack(conv_s)
    k_rows_prompt = jnp.stack(kp_rows)
    v_rows_prompt = jnp.stack(vp_rows)
    k_rows_sample = jnp.stack(ks_rows)
    v_rows_sample = jnp.stack(vs_rows)
    gmlp_v_sample = jnp.stack(gv_s)
    return (y_prompt, y_sample, conv_state_prompt, conv_state_sample, k_rows_prompt, v_rows_prompt,
            k_rows_sample, v_rows_sample, gmlp_v_sample)
```

```python
import functools

import numpy as np
import jax
import jax.numpy as jnp
from jax import lax
from jax.experimental import pallas as pl
from jax.experimental.pallas import tpu as pltpu

F32 = jnp.float32
BF16 = jnp.bfloat16

D_MODEL = 2048
BATCH = 2
SEQ = 4096
DEPTH = 4
DEC_BATCH = 16
DEC_SEQ = 32
CHUNK = 64
CONV_DIM = 1024
CONV_WIDTH = 31
CONV_HIST = CONV_WIDTH - 1
GMLP_DIM = 1024
GMLP_GROUPS = 8
GMLP_GC = GMLP_DIM // GMLP_GROUPS
GMLP_CHUNK = 128
N_HEADS = 8
HEAD_DIM = 128
ATTN_DIM = N_HEADS * HEAD_DIM
LEFT_CHUNKS = 8
ATTN_REACH = LEFT_CHUNKS * CHUNK
REL_CLIP = 128
D_FF = 4 * D_MODEL
PLE_DIM = 256
EPS = 1e-6
SCALE = HEAD_DIM ** -0.5
NEG = -1e30

ROWS_P = BATCH * SEQ
ROWS_S = DEC_BATCH * DEC_SEQ
ROWS = ROWS_P + ROWS_S
TM = 512
NT = ROWS // TM
NT_P = ROWS_P // TM
TILES_PER_SEQ = SEQ // TM
TM_MERGE = 256
TF = 1024
COLB = 1024
QB = 256
KB = 3 * QB
CONV_RC = 64
CONV_LB = 256
MIB = 1 << 20


def _rms(x, g):
    return x * lax.rsqrt(jnp.mean(x * x, axis=-1, keepdims=True) + EPS) * g


def _ln(x, g, b):
    mu = jnp.mean(x, axis=-1, keepdims=True)
    xc = x - mu
    var = jnp.mean(xc * xc, axis=-1, keepdims=True)
    return xc * lax.rsqrt(var + EPS) * g + b


def _dot(a, b):
    return jnp.dot(a, b, preferred_element_type=F32)


def _dot_nt(a, b):
    return lax.dot_general(a, b, (((1,), (1,)), ((), ())), preferred_element_type=F32)


def _call(body, grid, in_specs, out_specs, out_shape, *, name, vmem_mb, scratch=(), aliases=None):
    return pl.pallas_call(
        body,
        grid=grid,
        in_specs=in_specs,
        out_specs=out_specs,
        out_shape=out_shape,
        scratch_shapes=list(scratch),
        input_output_aliases=aliases or {},
        compiler_params=pltpu.CompilerParams(
            dimension_semantics=("arbitrary",) * len(grid), vmem_limit_bytes=vmem_mb * MIB),
        name=name,
    )


def _rows(cols, tm=TM, cb=0):
    return pl.BlockSpec((tm, cols), lambda i, *_: (i, cb))


def _resident(block, index):
    return pl.BlockSpec(block, lambda *_: index, pipeline_mode=pl.Buffered(1))


def _w(l, rows, cols, cb=0):
    return _resident((None, rows, cols), (l, 0, cb))


def _vec(l, n):
    return _resident((None, 1, n), (l, 0, 0))


def _rms_body(h_ref, g_ref, o_ref):
    o_ref[...] = _rms(h_ref[...], g_ref[...]).astype(o_ref.dtype)


def _rms_call(h, g3, l):
    return _call(
        _rms_body, (NT,), [_rows(D_MODEL), _vec(l, D_MODEL)], _rows(D_MODEL),
        jax.ShapeDtypeStruct((ROWS, D_MODEL), BF16), name="rms_first", vmem_mb=32)(h, g3)


def _glu_body(x_ref, wa_ref, wg_ref, a_ref):
    x = x_ref[...]
    a_ref[...] = _dot(x, wa_ref[...]) * jax.nn.sigmoid(_dot(x, wg_ref[...]))


def _glu_call(xn, w_in, l):
    return _call(
        _glu_body, (NT,),
        [_rows(D_MODEL), _w(l, D_MODEL, COLB, 0), _w(l, D_MODEL, COLB, 1)],
        _rows(CONV_DIM), jax.ShapeDtypeStruct((ROWS, CONV_DIM), F32),
        name="inproj_glu", vmem_mb=40)(xn, w_in, w_in)


def _gmlp_in_body(x_ref, wu_ref, wv_ref, g_ref, b_ref, u_ref, v_ref):
    x = x_ref[...]
    u_ref[...] = jax.nn.gelu(_dot(x, wu_ref[...])).astype(BF16)
    v_ref[...] = _ln(jax.nn.gelu(_dot(x, wv_ref[...])), g_ref[...], b_ref[...])


def _gmlp_in_call(xn, w_in, ln_g, ln_b, l):
    return _call(
        _gmlp_in_body, (NT,),
        [_rows(D_MODEL), _w(l, D_MODEL, COLB, 2), _w(l, D_MODEL, COLB, 3),
         _vec(l, GMLP_DIM), _vec(l, GMLP_DIM)],
        [_rows(GMLP_DIM), _rows(GMLP_DIM)],
        [jax.ShapeDtypeStruct((ROWS, GMLP_DIM), BF16), jax.ShapeDtypeStruct((ROWS, GMLP_DIM), F32)],
        name="inproj_gmlp", vmem_mb=40)(xn, w_in, w_in, ln_g, ln_b)


_KV_TILES = tuple(b * TILES_PER_SEQ + TILES_PER_SEQ - 1 for b in range(BATCH)) + (NT_P,)
_KV_SPARE = len(_KV_TILES)


def _kv_slot(i, *_):
    slot = _KV_SPARE
    for s, t in enumerate(_KV_TILES):
        slot = jnp.where(i == t, s, slot)
    return slot, 0


def _qkv_body(x_ref, wq_ref, wk_ref, wv_ref, q_ref, k_ref, v_ref, kv32_ref):
    x = x_ref[...]
    q_ref[...] = _dot(x, wq_ref[...]).astype(BF16)
    k = _dot(x, wk_ref[...])
    k_ref[...] = k.astype(BF16)
    kv32_ref[:, :ATTN_DIM] = k
    v = _dot(x, wv_ref[...])
    v_ref[...] = v.astype(BF16)
    kv32_ref[:, ATTN_DIM:] = v


def _qkv_call(xn, w_in, l):
    act = jax.ShapeDtypeStruct((ROWS, ATTN_DIM), BF16)
    return _call(
        _qkv_body, (NT,),
        [_rows(D_MODEL), _w(l, D_MODEL, COLB, 4), _w(l, D_MODEL, COLB, 5), _w(l, D_MODEL, COLB, 6)],
        [_rows(ATTN_DIM), _rows(ATTN_DIM), _rows(ATTN_DIM), pl.BlockSpec((TM, 2 * ATTN_DIM), _kv_slot)],
        [act, act, act, jax.ShapeDtypeStruct(((_KV_SPARE + 1) * TM, 2 * ATTN_DIM), F32)],
        name="inproj_qkv", vmem_mb=48)(xn, w_in, w_in, w_in)


def _gate_body(x_ref, *refs):
    w_refs, o_ref = refs[:-1], refs[-1]
    x = x_ref[...]
    for j, w_ref in enumerate(w_refs):
        o_ref[:, j * COLB:(j + 1) * COLB] = jax.nn.sigmoid(_dot(x, w_ref[...])).astype(BF16)


def _gate_call(xn, w_in, l):
    nblk = 3 * D_MODEL // COLB
    return _call(
        _gate_body, (NT,),
        [_rows(D_MODEL)] + [_w(l, D_MODEL, COLB, 7 + j) for j in range(nblk)],
        _rows(3 * D_MODEL), jax.ShapeDtypeStruct((ROWS, 3 * D_MODEL), BF16),
        name="inproj_gates", vmem_mb=56)(xn, *([w_in] * nblk))


def _merge_body(a_ref, b_ref, c_ref, g0_ref, g1_ref, g2_ref, h_ref,
                wa_ref, wb_ref, wc_ref, wo_ref, gf_ref, h1_ref, hn_ref):
    m = g0_ref[...].astype(F32) * _dot(a_ref[...], wa_ref[...])
    m = m + g1_ref[...].astype(F32) * _dot(b_ref[...], wb_ref[...])
    m = m + g2_ref[...].astype(F32) * _dot(c_ref[...], wc_ref[...])
    h1 = h_ref[...] + _dot(m.astype(BF16), wo_ref[...])
    h1_ref[...] = h1
    hn_ref[...] = _rms(h1, gf_ref[...]).astype(BF16)


def _merge_call(a, b, c, gates, h, w_a, w_b, w_c, w_o, g_ffn, l):
    tm = TM_MERGE
    return _call(
        _merge_body, (ROWS // tm,),
        [_rows(CONV_DIM, tm), _rows(GMLP_DIM, tm), _rows(ATTN_DIM, tm),
         _rows(D_MODEL, tm, 0), _rows(D_MODEL, tm, 1), _rows(D_MODEL, tm, 2), _rows(D_MODEL, tm),
         _w(l, CONV_DIM, D_MODEL), _w(l, GMLP_DIM, D_MODEL), _w(l, ATTN_DIM, D_MODEL),
         _w(l, D_MODEL, D_MODEL), _vec(l, D_MODEL)],
        [_rows(D_MODEL, tm), _rows(D_MODEL, tm)],
        [jax.ShapeDtypeStruct((ROWS, D_MODEL), F32), jax.ShapeDtypeStruct((ROWS, D_MODEL), BF16)],
        name="merge_out", vmem_mb=56)(a, b, c, gates, gates, gates, h, w_a, w_b, w_c, w_o, g_ffn)


def _ffn_body(hn_ref, w1_ref, w2_ref, h1_ref, o_ref):
    @pl.when(pl.program_id(1) == 0)
    def _():
        o_ref[...] = h1_ref[...]

    a = jnp.square(jnp.maximum(_dot(hn_ref[...], w1_ref[...]), 0.0)).astype(BF16)
    o_ref[...] += _dot(a, w2_ref[...])


def _ffn_call(hn, h1, w1, w2, l):
    return _call(
        _ffn_body, (NT, D_FF // TF),
        [pl.BlockSpec((TM, D_MODEL), lambda i, f: (i, 0)),
         pl.BlockSpec((None, D_MODEL, TF), lambda i, f: (l, 0, f)),
         pl.BlockSpec((None, TF, D_MODEL), lambda i, f: (l, f, 0)),
         pl.BlockSpec((TM, D_MODEL), lambda i, f: (i, 0))],
        pl.BlockSpec((TM, D_MODEL), lambda i, f: (i, 0)),
        jax.ShapeDtypeStruct((ROWS, D_MODEL), F32),
        name="ffn", vmem_mb=56)(hn, w1, w2, h1)


def _ple_body(h2_ref, p_ref, wg_ref, wp_ref, gp_ref, gn_ref, h3_ref, xn_ref):
    h2 = h2_ref[...]
    gate = jax.nn.sigmoid(_dot(_rms(h2, gp_ref[...]).astype(BF16), wg_ref[...]))
    h3 = h2 + gate * _dot(p_ref[...].astype(BF16), wp_ref[...])
    h3_ref[...] = h3
    xn_ref[...] = _rms(h3, gn_ref[...]).astype(xn_ref.dtype)


def _ple_call(h2, p_all, w_g, w_p, g_ple, g_next, l, l_next, out_dtype):
    return _call(
        _ple_body, (NT,),
        [_rows(D_MODEL), pl.BlockSpec((None, TM, PLE_DIM), lambda i: (l, i, 0)),
         _w(l, D_MODEL, D_MODEL), _w(l, PLE_DIM, D_MODEL), _vec(l, D_MODEL), _vec(l_next, D_MODEL)],
        [_rows(D_MODEL), _rows(D_MODEL)],
        [jax.ShapeDtypeStruct((ROWS, D_MODEL), F32), jax.ShapeDtypeStruct((ROWS, D_MODEL), out_dtype)],
        name="ple", vmem_mb=56)(h2, p_all, w_g, w_p, g_ple, g_next)


def _conv_rows(ext_ref, y_ref, cw_ref, ext_base, y_base, rc):
    span = rc + 32
    for lb in range(CONV_DIM // CONV_LB):
        ls = slice(lb * CONV_LB, (lb + 1) * CONV_LB)
        x = ext_ref[pl.ds(ext_base, span), ls]
        acc = jnp.zeros((rc, CONV_LB), F32)
        for j in range(CONV_WIDTH):
            acc = acc + cw_ref[j:j + 1, ls] * x[2 + j:2 + j + rc, :]
        y_ref[pl.ds(y_base, rc), ls] = acc


def _conv_body(a_ref, cache_ref, cw_ref, cb_ref, g_ref, b_ref, o_ref, ext_ref, y_ref):
    i = pl.program_id(0)
    hist = 32

    @pl.when(i < NT_P)
    def _():
        @pl.when(i % TILES_PER_SEQ == 0)
        def _():
            ext_ref[0:hist, :] = jnp.zeros((hist, CONV_DIM), F32)

        @pl.when(i % TILES_PER_SEQ != 0)
        def _():
            ext_ref[0:hist, :] = ext_ref[TM:TM + hist, :]

        ext_ref[hist:hist + TM, :] = a_ref[...]

        def chunk(c, carry):
            r0 = pl.multiple_of(c * CONV_RC, CONV_RC)
            _conv_rows(ext_ref, y_ref, cw_ref, r0, r0, CONV_RC)
            return carry

        lax.fori_loop(0, TM // CONV_RC, chunk, 0)

    @pl.when(i == NT_P)
    def _():
        def seq(s, carry):
            r0 = pl.multiple_of(s * DEC_SEQ, DEC_SEQ)
            ext_ref[hist - CONV_HIST:hist, :] = cache_ref[s]
            ext_ref[hist:hist + DEC_SEQ, :] = a_ref[pl.ds(r0, DEC_SEQ), :]
            _conv_rows(ext_ref, y_ref, cw_ref, 0, r0, DEC_SEQ)
            return carry

        lax.fori_loop(0, DEC_BATCH, seq, 0)

    y = _ln(y_ref[...] + cb_ref[...], g_ref[...], b_ref[...])
    o_ref[...] = (y * jax.nn.sigmoid(y)).astype(BF16)


def _conv_call(a, cache_conv, conv_w, conv_b, ln_g, ln_b, l):
    return _call(
        _conv_body, (NT,),
        [_rows(CONV_DIM),
         _resident((None, DEC_BATCH, CONV_HIST, CONV_DIM), (l, 0, 0, 0)),
         _resident((None, CONV_WIDTH, CONV_DIM), (l, 0, 0)),
         _vec(l, CONV_DIM), _vec(l, CONV_DIM), _vec(l, CONV_DIM)],
        _rows(CONV_DIM), jax.ShapeDtypeStruct((ROWS, CONV_DIM), BF16),
        scratch=[pltpu.VMEM((TM + 32, CONV_DIM), F32), pltpu.VMEM((TM, CONV_DIM), F32)],
        name="conv_mixer", vmem_mb=32)(a, cache_conv, conv_w, conv_b, ln_g, ln_b)


def _gmlp_mix_body(u_ref, v_ref, w_ref, bias_ref, o_ref):
    shift = jnp.where(pl.program_id(0) == NT_P, 5, 7)
    r = lax.broadcasted_iota(jnp.int32, (GMLP_CHUNK, GMLP_CHUNK), 0)
    c = lax.broadcasted_iota(jnp.int32, (GMLP_CHUNK, GMLP_CHUNK), 1)
    mask = (r >= c) & (lax.shift_right_logical(r, shift) == lax.shift_right_logical(c, shift))
    for g in range(GMLP_GROUPS):
        cs = slice(g * GMLP_GC, (g + 1) * GMLP_GC)
        wg = jnp.where(mask, w_ref[g], 0.0).astype(BF16)
        bias = bias_ref[:, cs]
        for k in range(TM // GMLP_CHUNK):
            rs = slice(k * GMLP_CHUNK, (k + 1) * GMLP_CHUNK)
            mixed = _dot(wg, v_ref[rs, cs].astype(BF16)) + bias
            o_ref[rs, cs] = (u_ref[rs, cs].astype(F32) * mixed).astype(BF16)


def _gmlp_mix_call(u, v, wmix, bmix):
    pick = lambda i: (jnp.where(i == NT_P, 1, 0), 0, 0, 0)
    return _call(
        _gmlp_mix_body, (NT,),
        [_rows(GMLP_DIM), _rows(GMLP_DIM),
         pl.BlockSpec((None, GMLP_GROUPS, GMLP_CHUNK, GMLP_CHUNK), pick),
         pl.BlockSpec((None, GMLP_CHUNK, GMLP_DIM), lambda i: pick(i)[:3])],
        _rows(GMLP_DIM), jax.ShapeDtypeStruct((ROWS, GMLP_DIM), BF16),
        name="gmlp_mix", vmem_mb=32)(u, v, wmix, bmix)


def _attn_p_body(q_ref, k0_ref, k1_ref, k2_ref, v0_ref, v1_ref, v2_ref, bias_ref, o_ref):
    m = pl.program_id(1)
    k_refs = (k0_ref, k1_ref, k2_ref)
    v_refs = (v0_ref, v1_ref, v2_ref)
    pen = [jnp.where(m - 2 + s >= 0, 0.0, NEG).astype(F32) for s in range(3)]
    for h in range(N_HEADS):
        hs = slice(h * HEAD_DIM, (h + 1) * HEAD_DIM)
        qh = q_ref[:, hs]
        sc = [_dot_nt(qh, k_refs[s][:, hs]) * SCALE + bias_ref[h, :, s * QB:(s + 1) * QB] + pen[s]
              for s in range(3)]
        mx = functools.reduce(jnp.maximum, [jnp.max(x, axis=-1, keepdims=True) for x in sc])
        e = [jnp.exp(x - mx) for x in sc]
        den = functools.reduce(jnp.add, [jnp.sum(x, axis=-1, keepdims=True) for x in e])
        o = functools.reduce(jnp.add, [_dot(e[s].astype(BF16), v_refs[s][:, hs]) for s in range(3)])
        o_ref[:, hs] = (o * (1.0 / den)).astype(BF16)


def _attn_p_call(q, k, v, bias):
    nq = SEQ // QB

    def kv_spec(s):
        return pl.BlockSpec((QB, ATTN_DIM), lambda b, m: (b * nq + jnp.maximum(m - 2 + s, 0), 0))

    return _call(
        _attn_p_body, (BATCH, nq),
        [pl.BlockSpec((QB, ATTN_DIM), lambda b, m: (b * nq + m, 0))]
        + [kv_spec(s) for s in range(3)] + [kv_spec(s) for s in range(3)]
        + [_resident((N_HEADS, QB, KB), (0, 0, 0))],
        pl.BlockSpec((QB, ATTN_DIM), lambda b, m: (b * nq + m, 0)),
        jax.ShapeDtypeStruct((ROWS, ATTN_DIM), BF16),
        name="attn_prompt", vmem_mb=40)(q, k, k, k, v, v, v, bias)


def _attn_s_body(q_ref, kn_ref, vn_ref, kc_ref, vc_ref, bias_ref, o_in_ref, o_ref):
    del o_in_ref
    w = ATTN_REACH
    for h in range(N_HEADS):
        hs = slice(h * HEAD_DIM, (h + 1) * HEAD_DIM)
        qh = q_ref[:, hs]
        s_c = _dot_nt(qh, kc_ref[:, hs].astype(BF16)) * SCALE + bias_ref[h, :, :w]
        s_n = _dot_nt(qh, kn_ref[:, hs]) * SCALE + bias_ref[h, :, w:]
        mx = jnp.maximum(jnp.max(s_c, axis=-1, keepdims=True), jnp.max(s_n, axis=-1, keepdims=True))
        e_c = jnp.exp(s_c - mx)
        e_n = jnp.exp(s_n - mx)
        den = jnp.sum(e_c, axis=-1, keepdims=True) + jnp.sum(e_n, axis=-1, keepdims=True)
        o = _dot(e_c.astype(BF16), vc_ref[:, hs].astype(BF16)) + _dot(e_n.astype(BF16), vn_ref[:, hs])
        o_ref[:, hs] = (o * (1.0 / den)).astype(BF16)


def _attn_s_call(q, k, v, cache_k, cache_v, bias, o_prompt, l):
    first = ROWS_P // DEC_SEQ
    new = pl.BlockSpec((DEC_SEQ, ATTN_DIM), lambda b: (first + b, 0))
    cache = pl.BlockSpec((None, None, ATTN_REACH, ATTN_DIM), lambda b: (l, b, 0, 0))
    return _call(
        _attn_s_body, (DEC_BATCH,),
        [new, new, new, cache, cache, _resident((N_HEADS, DEC_SEQ, ATTN_REACH + DEC_SEQ), (0, 0, 0)),
         pl.BlockSpec(memory_space=pl.ANY)],
        new, jax.ShapeDtypeStruct((ROWS, ATTN_DIM), BF16),
        aliases={6: 0}, name="attn_sample", vmem_mb=32)(q, k, v, cache_k, cache_v, bias, o_prompt)


def _band_bias(rel_tab):
    r = np.arange(QB)[:, None]
    j = np.arange(KB)[None, :]
    idx = np.clip(r - (j - 2 * QB), -REL_CLIP, REL_CLIP) + REL_CLIP
    kc = j // CHUNK - 2 * QB // CHUNK
    qc = r // CHUNK
    band = (kc >= qc - LEFT_CHUNKS) & (kc <= qc)
    return jnp.where(band[None], rel_tab[:, idx], NEG).astype(F32)


def _sample_bias(rel_tab):
    w = ATTN_REACH
    rel = np.arange(DEC_SEQ)[:, None] + w - np.arange(w + DEC_SEQ)[None, :]
    return rel_tab[:, np.clip(rel, -REL_CLIP, REL_CLIP) + REL_CLIP].astype(F32)


def kernel(x_prompt, x_sample, p_prompt, p_sample, cache_conv, cache_k, cache_v, norm_mix, w_in, conv_w, conv_b, conv_ln_g, conv_ln_b, w_a_out, gmlp_ln_g, gmlp_ln_b, gmlp_ws, gmlp_bs, w_b_out, attn_rel_bias, w_c_out, w_o, norm_ffn, w_ff1, w_ff2, norm_ple, w_ple_gate, w_ple_proj, norm_final):
    h = jnp.concatenate([x_prompt.reshape(ROWS_P, D_MODEL), x_sample.reshape(ROWS_S, D_MODEL)], axis=0)
    p_all = jnp.concatenate([p_prompt.reshape(DEPTH, ROWS_P, PLE_DIM),
                             p_sample.reshape(DEPTH, ROWS_S, PLE_DIM)], axis=1)
    cache_k = cache_k.reshape(DEPTH, DEC_BATCH, ATTN_REACH, ATTN_DIM)
    cache_v = cache_v.reshape(DEPTH, DEC_BATCH, ATTN_REACH, ATTN_DIM)

    vec = lambda x: x.reshape(x.shape[0], 1, x.shape[-1])
    g_mix = vec(jnp.concatenate([norm_mix, norm_final[None]], axis=0))
    conv_b3, conv_g3, conv_lb3 = vec(conv_b), vec(conv_ln_g), vec(conv_ln_b)
    gm_g3, gm_b3 = vec(gmlp_ln_g), vec(gmlp_ln_b)
    g_ffn3, g_ple3 = vec(norm_ffn), vec(norm_ple)

    w_in_b, w_a_b, w_b_b, w_c_b, w_o_b = (x.astype(BF16) for x in (w_in, w_a_out, w_b_out, w_c_out, w_o))
    w_ff1_b, w_ff2_b, w_pg_b, w_pp_b = (x.astype(BF16) for x in (w_ff1, w_ff2, w_ple_gate, w_ple_proj))

    outs = [[] for _ in range(7)]
    xn = _rms_call(h, g_mix, 0)
    for l in range(DEPTH):
        a = _glu_call(xn, w_in_b, l)
        u, vg = _gmlp_in_call(xn, w_in_b, gm_g3, gm_b3, l)
        q, k, v, kv32 = _qkv_call(xn, w_in_b, l)
        gates = _gate_call(xn, w_in_b, l)

        ya_in = _conv_call(a, cache_conv, conv_w, conv_b3, conv_g3, conv_lb3, l)

        ws = gmlp_ws[l]
        corner = jnp.tile(ws[:, :DEC_SEQ, :DEC_SEQ], (1, GMLP_CHUNK // DEC_SEQ, GMLP_CHUNK // DEC_SEQ))
        bias_p = jnp.repeat(gmlp_bs[l].T, GMLP_GC, axis=1)
        bias_s = jnp.tile(bias_p[:DEC_SEQ], (GMLP_CHUNK // DEC_SEQ, 1))
        yb_in = _gmlp_mix_call(u, vg, jnp.stack([ws, corner]), jnp.stack([bias_p, bias_s]))

        rel_tab = attn_rel_bias[l]
        o = _attn_p_call(q, k, v, _band_bias(rel_tab))
        o = _attn_s_call(q, k, v, cache_k, cache_v, _sample_bias(rel_tab), o, l)

        h1, hn = _merge_call(ya_in, yb_in, o, gates, h, w_a_b, w_b_b, w_c_b, w_o_b, g_ffn3, l)
        h2 = _ffn_call(hn, h1, w_ff1_b, w_ff2_b, l)
        last = l == DEPTH - 1
        h, xn = _ple_call(h2, p_all, w_pg_b, w_pp_b, g_ple3, g_mix, l, l + 1, F32 if last else BF16)

        a_p = a[:ROWS_P].reshape(BATCH, SEQ, CONV_DIM)
        a_s = a[ROWS_P:].reshape(DEC_BATCH, DEC_SEQ, CONV_DIM)
        outs[0].append(a_p[:, SEQ - CONV_HIST:])
        outs[1].append(a_s[:, DEC_SEQ - CONV_HIST:])
        kv_p = kv32[:BATCH * TM].reshape(BATCH, TM, 2, N_HEADS, HEAD_DIM)
        kv_s = kv32[BATCH * TM:(BATCH + 1) * TM].reshape(DEC_BATCH, DEC_SEQ, 2, N_HEADS, HEAD_DIM)
        outs[2].append(kv_p[:, :, 0])
        outs[3].append(kv_p[:, :, 1])
        outs[4].append(kv_s[:, :, 0])
        outs[5].append(kv_s[:, :, 1])
        outs[6].append(vg[ROWS_P:].reshape(DEC_BATCH, DEC_SEQ, GMLP_DIM))

    y_prompt = xn[:ROWS_P].reshape(BATCH, SEQ, D_MODEL)
    y_sample = xn[ROWS_P:].reshape(DEC_BATCH, DEC_SEQ, D_MODEL)
    return (y_prompt, y_sample) + tuple(jnp.stack(x) for x in outs)
```

```python
import functools

import jax
import jax.numpy as jnp
from jax import lax
from jax.experimental import pallas as pl
from jax.experimental.pallas import tpu as pltpu

F32 = jnp.float32
BF16 = jnp.bfloat16

D_MODEL = 2048
BATCH = 2
SEQ = 4096
DEPTH = 4
DEC_BATCH = 16
DEC_SEQ = 32
CHUNK = 64
CONV_DIM = 1024
CONV_WIDTH = 31
CONV_HIST = CONV_WIDTH - 1
GMLP_DIM = 1024
GMLP_GROUPS = 8
GMLP_GC = GMLP_DIM // GMLP_GROUPS
GMLP_CHUNK = 128
N_HEADS = 8
HEAD_DIM = 128
ATTN_DIM = N_HEADS * HEAD_DIM
LEFT_CHUNKS = 8
ATTN_REACH = LEFT_CHUNKS * CHUNK
REL_CLIP = 128
D_FF = 4 * D_MODEL
PLE_DIM = 256
EPS = 1e-6
SCALE = HEAD_DIM ** -0.5
NEG = -1e30

ROWS_P = BATCH * SEQ
ROWS_S = DEC_BATCH * DEC_SEQ
ROWS = ROWS_P + ROWS_S
TM = 512
NT = ROWS // TM
NT_P = ROWS_P // TM
TILES_PER_SEQ = SEQ // TM
TM_MERGE = 256
TF = 1024
COLB = 1024
QB = 256
KB = 3 * QB
CONV_RC = 64
CONV_LB = 128
SUBLANES = 8
TOEP = 1024
MIB = 1 << 20


def _rms(x, g):
    return x * lax.rsqrt(jnp.mean(x * x, axis=-1, keepdims=True) + EPS) * g


def _ln(x, g, b):
    mu = jnp.mean(x, axis=-1, keepdims=True)
    xc = x - mu
    var = jnp.mean(xc * xc, axis=-1, keepdims=True)
    return xc * lax.rsqrt(var + EPS) * g + b


def _dot(a, b):
    return jnp.dot(a, b, preferred_element_type=F32)


def _dot_nt(a, b):
    return lax.dot_general(a, b, (((1,), (1,)), ((), ())), preferred_element_type=F32)


def _call(body, grid, in_specs, out_specs, out_shape, *, name, vmem_mb, scratch=(), aliases=None):
    return pl.pallas_call(
        body,
        grid=grid,
        in_specs=in_specs,
        out_specs=out_specs,
        out_shape=out_shape,
        scratch_shapes=list(scratch),
        input_output_aliases=aliases or {},
        compiler_params=pltpu.CompilerParams(
            dimension_semantics=("arbitrary",) * len(grid), vmem_limit_bytes=vmem_mb * MIB),
        name=name,
    )


def _rows(cols, tm=TM, cb=0):
    return pl.BlockSpec((tm, cols), lambda i, *_: (i, cb))


def _resident(block, index):
    return pl.BlockSpec(block, lambda *_: index, pipeline_mode=pl.Buffered(1))


def _w(l, rows, cols, cb=0):
    return _resident((None, rows, cols), (l, 0, cb))


def _vec(l, n):
    return _resident((None, 1, n), (l, 0, 0))


def _rms_body(h_ref, g_ref, o_ref):
    o_ref[...] = _rms(h_ref[...], g_ref[...]).astype(o_ref.dtype)


def _rms_call(h, g3, l):
    return _call(
        _rms_body, (NT,), [_rows(D_MODEL), _vec(l, D_MODEL)], _rows(D_MODEL),
        jax.ShapeDtypeStruct((ROWS, D_MODEL), BF16), name="rms_first", vmem_mb=32)(h, g3)


def _glu_body(x_ref, wa_ref, wg_ref, a_ref):
    x = x_ref[...]
    a_ref[...] = _dot(x, wa_ref[...]) * jax.nn.sigmoid(_dot(x, wg_ref[...]))


def _glu_call(xn, w_in, l):
    return _call(
        _glu_body, (NT,),
        [_rows(D_MODEL), _w(l, D_MODEL, COLB, 0), _w(l, D_MODEL, COLB, 1)],
        _rows(CONV_DIM), jax.ShapeDtypeStruct((ROWS, CONV_DIM), F32),
        name="inproj_glu", vmem_mb=40)(xn, w_in, w_in)


def _gmlp_in_body(x_ref, wu_ref, wv_ref, g_ref, b_ref, u_ref, v_ref):
    x = x_ref[...]
    v_ref[...] = _ln(jax.nn.gelu(_dot(x, wv_ref[...])), g_ref[...], b_ref[...])
    u_ref[...] = jax.nn.gelu(_dot(x, wu_ref[...])).astype(BF16)


def _gmlp_in_call(xn, w_in, ln_g, ln_b, l):
    return _call(
        _gmlp_in_body, (NT,),
        [_rows(D_MODEL), _w(l, D_MODEL, COLB, 2), _w(l, D_MODEL, COLB, 3),
         _vec(l, GMLP_DIM), _vec(l, GMLP_DIM)],
        [_rows(GMLP_DIM), _rows(GMLP_DIM)],
        [jax.ShapeDtypeStruct((ROWS, GMLP_DIM), BF16), jax.ShapeDtypeStruct((ROWS, GMLP_DIM), F32)],
        name="inproj_gmlp", vmem_mb=40)(xn, w_in, w_in, ln_g, ln_b)


_KV_TILES = tuple(b * TILES_PER_SEQ + TILES_PER_SEQ - 1 for b in range(BATCH)) + (NT_P,)
_KV_SPARE = len(_KV_TILES)


def _kv_slot(i, *_):
    slot = _KV_SPARE
    for s, t in enumerate(_KV_TILES):
        slot = jnp.where(i == t, s, slot)
    return slot, 0


def _qkv_body(x_ref, wq_ref, wk_ref, wv_ref, q_ref, k_ref, v_ref, kv32_ref):
    x = x_ref[...]
    q_ref[...] = (_dot(x, wq_ref[...]) * SCALE).astype(BF16)
    k = _dot(x, wk_ref[...])
    k_ref[...] = k.astype(BF16)
    kv32_ref[:, :ATTN_DIM] = k
    v = _dot(x, wv_ref[...])
    v_ref[...] = v.astype(BF16)
    kv32_ref[:, ATTN_DIM:] = v


def _qkv_call(xn, w_in, l):
    act = jax.ShapeDtypeStruct((ROWS, ATTN_DIM), BF16)
    return _call(
        _qkv_body, (NT,),
        [_rows(D_MODEL), _w(l, D_MODEL, COLB, 4), _w(l, D_MODEL, COLB, 5), _w(l, D_MODEL, COLB, 6)],
        [_rows(ATTN_DIM), _rows(ATTN_DIM), _rows(ATTN_DIM), pl.BlockSpec((TM, 2 * ATTN_DIM), _kv_slot)],
        [act, act, act, jax.ShapeDtypeStruct(((_KV_SPARE + 1) * TM, 2 * ATTN_DIM), F32)],
        name="inproj_qkv", vmem_mb=48)(xn, w_in, w_in, w_in)


def _gate_body(x_ref, *refs):
    w_refs, o_ref = refs[:-1], refs[-1]
    x = x_ref[...]
    for j, w_ref in enumerate(w_refs):
        o_ref[:, j * COLB:(j + 1) * COLB] = jax.nn.sigmoid(_dot(x, w_ref[...])).astype(BF16)


def _gate_call(xn, w_in, l):
    nblk = 3 * D_MODEL // COLB
    return _call(
        _gate_body, (NT,),
        [_rows(D_MODEL)] + [_w(l, D_MODEL, COLB, 7 + j) for j in range(nblk)],
        _rows(3 * D_MODEL), jax.ShapeDtypeStruct((ROWS, 3 * D_MODEL), BF16),
        name="inproj_gates", vmem_mb=56)(xn, *([w_in] * nblk))


def _merge_body(a_ref, b_ref, c_ref, g0_ref, g1_ref, g2_ref, h_ref,
                wa_ref, wb_ref, wc_ref, wo_ref, gf_ref, h1_ref, hn_ref):
    m = g0_ref[...].astype(F32) * _dot(a_ref[...], wa_ref[...])
    m = m + g1_ref[...].astype(F32) * _dot(b_ref[...], wb_ref[...])
    m = m + g2_ref[...].astype(F32) * _dot(c_ref[...], wc_ref[...])
    h1 = h_ref[...] + _dot(m.astype(BF16), wo_ref[...])
    h1_ref[...] = h1
    hn_ref[...] = _rms(h1, gf_ref[...]).astype(BF16)


def _merge_call(a, b, c, gates, h, w_a, w_b, w_c, w_o, g_ffn, l):
    tm = TM_MERGE
    return _call(
        _merge_body, (ROWS // tm,),
        [_rows(CONV_DIM, tm), _rows(GMLP_DIM, tm), _rows(ATTN_DIM, tm),
         _rows(D_MODEL, tm, 0), _rows(D_MODEL, tm, 1), _rows(D_MODEL, tm, 2), _rows(D_MODEL, tm),
         _w(l, CONV_DIM, D_MODEL), _w(l, GMLP_DIM, D_MODEL), _w(l, ATTN_DIM, D_MODEL),
         _w(l, D_MODEL, D_MODEL), _vec(l, D_MODEL)],
        [_rows(D_MODEL, tm), _rows(D_MODEL, tm)],
        [jax.ShapeDtypeStruct((ROWS, D_MODEL), F32), jax.ShapeDtypeStruct((ROWS, D_MODEL), BF16)],
        name="merge_out", vmem_mb=56)(a, b, c, gates, gates, gates, h, w_a, w_b, w_c, w_o, g_ffn)


def _ffn_body(hn_ref, w1_ref, w2_ref, h1_ref, o_ref):
    @pl.when(pl.program_id(1) == 0)
    def _():
        o_ref[...] = h1_ref[...]

    a = jnp.square(jnp.maximum(_dot(hn_ref[...], w1_ref[...]), 0.0)).astype(BF16)
    o_ref[...] += _dot(a, w2_ref[...])


def _ffn_call(hn, h1, w1, w2, l):
    return _call(
        _ffn_body, (NT, D_FF // TF),
        [pl.BlockSpec((TM, D_MODEL), lambda i, f: (i, 0)),
         pl.BlockSpec((None, D_MODEL, TF), lambda i, f: (l, 0, f)),
         pl.BlockSpec((None, TF, D_MODEL), lambda i, f: (l, f, 0)),
         pl.BlockSpec((TM, D_MODEL), lambda i, f: (i, 0))],
        pl.BlockSpec((TM, D_MODEL), lambda i, f: (i, 0)),
        jax.ShapeDtypeStruct((ROWS, D_MODEL), F32),
        name="ffn", vmem_mb=56)(hn, w1, w2, h1)


def _ple_body(h2_ref, p_ref, wg_ref, wp_ref, gp_ref, gn_ref, h3_ref, xn_ref):
    h2 = h2_ref[...]
    gate = jax.nn.sigmoid(_dot(_rms(h2, gp_ref[...]).astype(BF16), wg_ref[...]))
    h3 = h2 + gate * _dot(p_ref[...].astype(BF16), wp_ref[...])
    h3_ref[...] = h3
    xn_ref[...] = _rms(h3, gn_ref[...]).astype(xn_ref.dtype)


def _ple_call(h2, p_all, w_g, w_p, g_ple, g_next, l, l_next, out_dtype):
    return _call(
        _ple_body, (NT,),
        [_rows(D_MODEL), pl.BlockSpec((None, TM, PLE_DIM), lambda i: (l, i, 0)),
         _w(l, D_MODEL, D_MODEL), _w(l, PLE_DIM, D_MODEL), _vec(l, D_MODEL), _vec(l_next, D_MODEL)],
        [_rows(D_MODEL), _rows(D_MODEL)],
        [jax.ShapeDtypeStruct((ROWS, D_MODEL), F32), jax.ShapeDtypeStruct((ROWS, D_MODEL), out_dtype)],
        name="ple", vmem_mb=56)(h2, p_all, w_g, w_p, g_ple, g_next)


def _conv_rows(ext_ref, y_ref, cw_ref, ext_base, y_base, rc):
    for lb in range(CONV_DIM // CONV_LB):
        ls = slice(lb * CONV_LB, (lb + 1) * CONV_LB)
        acc = None
        for r in range(SUBLANES):
            rows = rc + (SUBLANES if r else 0)
            u = None
            for j in range(CONV_WIDTH):
                if (2 + j) % SUBLANES == r:
                    start = ext_base + (2 + j - r)
                    if not isinstance(start, int):
                        start = pl.multiple_of(start, SUBLANES)
                    term = cw_ref[j:j + 1, ls] * ext_ref[pl.ds(start, rows), ls]
                    u = term if u is None else u + term
            part = u[r:r + rc, :]
            acc = part if acc is None else acc + part
        y_ref[pl.ds(y_base, rc), ls] = acc


def _conv_body(a_ref, cache_ref, cw_ref, cb_ref, g_ref, b_ref, o_ref, ext_ref, y_ref):
    i = pl.program_id(0)
    hist = 32

    @pl.when(i < NT_P)
    def _():
        @pl.when(i % TILES_PER_SEQ == 0)
        def _():
            ext_ref[0:hist, :] = jnp.zeros((hist, CONV_DIM), F32)

        @pl.when(i % TILES_PER_SEQ != 0)
        def _():
            ext_ref[0:hist, :] = ext_ref[TM:TM + hist, :]

        ext_ref[hist:hist + TM, :] = a_ref[...]

        def chunk(c, carry):
            r0 = pl.multiple_of(c * CONV_RC, CONV_RC)
            _conv_rows(ext_ref, y_ref, cw_ref, r0, r0, CONV_RC)
            return carry

        lax.fori_loop(0, TM // CONV_RC, chunk, 0)

    @pl.when(i == NT_P)
    def _():
        def seq(s, carry):
            r0 = pl.multiple_of(s * DEC_SEQ, DEC_SEQ)
            ext_ref[hist - CONV_HIST:hist, :] = cache_ref[s]
            ext_ref[hist:hist + DEC_SEQ, :] = a_ref[pl.ds(r0, DEC_SEQ), :]
            _conv_rows(ext_ref, y_ref, cw_ref, 0, r0, DEC_SEQ)
            return carry

        lax.fori_loop(0, DEC_BATCH, seq, 0)

    y = _ln(y_ref[...] + cb_ref[...], g_ref[...], b_ref[...])
    o_ref[...] = (y * jax.nn.sigmoid(y)).astype(BF16)


def _conv_call(a, cache_conv, conv_w, conv_b, ln_g, ln_b, l):
    return _call(
        _conv_body, (NT,),
        [_rows(CONV_DIM),
         _resident((None, DEC_BATCH, CONV_HIST, CONV_DIM), (l, 0, 0, 0)),
         _resident((None, CONV_WIDTH, CONV_DIM), (l, 0, 0)),
         _vec(l, CONV_DIM), _vec(l, CONV_DIM), _vec(l, CONV_DIM)],
        _rows(CONV_DIM), jax.ShapeDtypeStruct((ROWS, CONV_DIM), BF16),
        scratch=[pltpu.VMEM((TM + 32, CONV_DIM), F32), pltpu.VMEM((TM, CONV_DIM), F32)],
        name="conv_mixer", vmem_mb=32)(a, cache_conv, conv_w, conv_b, ln_g, ln_b)


def _gmlp_mix_body(u_ref, v_ref, w_ref, bias_ref, o_ref):
    shift = jnp.where(pl.program_id(0) == NT_P, 5, 7)
    r = lax.broadcasted_iota(jnp.int32, (GMLP_CHUNK, GMLP_CHUNK), 0)
    c = lax.broadcasted_iota(jnp.int32, (GMLP_CHUNK, GMLP_CHUNK), 1)
    mask = (r >= c) & (lax.shift_right_logical(r, shift) == lax.shift_right_logical(c, shift))
    for g in range(GMLP_GROUPS):
        cs = slice(g * GMLP_GC, (g + 1) * GMLP_GC)
        wg = jnp.where(mask, w_ref[g], 0.0).astype(BF16)
        bias = bias_ref[:, cs]
        for k in range(TM // GMLP_CHUNK):
            rs = slice(k * GMLP_CHUNK, (k + 1) * GMLP_CHUNK)
            mixed = _dot(wg, v_ref[rs, cs].astype(BF16)) + bias
            o_ref[rs, cs] = (u_ref[rs, cs].astype(F32) * mixed).astype(BF16)


def _gmlp_mix_call(u, v, wmix, bmix):
    pick = lambda i: (jnp.where(i == NT_P, 1, 0), 0, 0, 0)
    return _call(
        _gmlp_mix_body, (NT,),
        [_rows(GMLP_DIM), _rows(GMLP_DIM),
         pl.BlockSpec((None, GMLP_GROUPS, GMLP_CHUNK, GMLP_CHUNK), pick),
         pl.BlockSpec((None, GMLP_CHUNK, GMLP_DIM), lambda i: pick(i)[:3])],
        _rows(GMLP_DIM), jax.ShapeDtypeStruct((ROWS, GMLP_DIM), BF16),
        name="gmlp_mix", vmem_mb=32)(u, v, wmix, bmix)


def _toeplitz_rows(tab_ref, h, rows):
    t = jnp.broadcast_to(tab_ref[h:h + 1, :], (rows, TOEP))
    return pltpu.roll(t, TOEP - QB + 1, 1, stride=1, stride_axis=0)


def _attn_p_body(q_ref, k0_ref, k1_ref, k2_ref, v0_ref, v1_ref, v2_ref, tab_ref, o_ref, bias_ref):
    m = pl.program_id(1)
    k_refs = (k0_ref, k1_ref, k2_ref)
    v_refs = (v0_ref, v1_ref, v2_ref)

    @pl.when((pl.program_id(0) == 0) & (m == 0))
    def _():
        r = lax.broadcasted_iota(jnp.int32, (QB, KB), 0) // CHUNK
        j = lax.broadcasted_iota(jnp.int32, (QB, KB), 1) // CHUNK
        band = (j >= r) & (j <= r + LEFT_CHUNKS)
        for h in range(N_HEADS):
            bias = jnp.where(band, _toeplitz_rows(tab_ref, h, QB)[:, :KB], NEG)
            for s in range(3):
                bias_ref[h, s] = bias[:, s * QB:(s + 1) * QB]
            bias_ref[h, 3] = jnp.full((QB, QB), NEG, F32)

    slot = [jnp.where(m - 2 + s >= 0, s, 3) for s in range(3)]
    for h in range(N_HEADS):
        hs = slice(h * HEAD_DIM, (h + 1) * HEAD_DIM)
        qh = q_ref[:, hs]
        sc = [_dot_nt(qh, k_refs[s][:, hs]) + bias_ref[h, slot[s]] for s in range(3)]
        mx = jnp.max(functools.reduce(jnp.maximum, sc), axis=-1, keepdims=True)
        e = [jnp.exp(x - mx) for x in sc]
        den = jnp.sum(functools.reduce(jnp.add, e), axis=-1, keepdims=True)
        o = functools.reduce(jnp.add, [_dot(e[s].astype(BF16), v_refs[s][:, hs]) for s in range(3)])
        o_ref[:, hs] = (o * (1.0 / den)).astype(BF16)


def _attn_p_call(q, k, v, toep, l):
    nq = SEQ // QB

    def kv_spec(s):
        return pl.BlockSpec((QB, ATTN_DIM), lambda b, m: (b * nq + jnp.maximum(m - 2 + s, 0), 0))

    return _call(
        _attn_p_body, (BATCH, nq),
        [pl.BlockSpec((QB, ATTN_DIM), lambda b, m: (b * nq + m, 0))]
        + [kv_spec(s) for s in range(3)] + [kv_spec(s) for s in range(3)]
        + [_resident((None, N_HEADS, TOEP), (l, 0, 0))],
        pl.BlockSpec((QB, ATTN_DIM), lambda b, m: (b * nq + m, 0)),
        jax.ShapeDtypeStruct((ROWS, ATTN_DIM), BF16),
        scratch=[pltpu.VMEM((N_HEADS, 4, QB, QB), F32)],
        name="attn_prompt", vmem_mb=40)(q, k, k, k, v, v, v, toep)


def _attn_s_body(q_ref, kn_ref, vn_ref, kc_ref, vc_ref, tab_ref, o_in_ref, o_ref, bias_ref):
    del o_in_ref
    w = ATTN_REACH

    @pl.when(pl.program_id(0) == 0)
    def _():
        for h in range(N_HEADS):
            bias_ref[h] = _toeplitz_rows(tab_ref, h, DEC_SEQ)[:, :w + DEC_SEQ]

    for h in range(N_HEADS):
        hs = slice(h * HEAD_DIM, (h + 1) * HEAD_DIM)
        rows_h = pl.ds(h, w, stride=N_HEADS)
        qh = q_ref[:, hs]
        s_c = _dot_nt(qh, kc_ref[rows_h, :].astype(BF16)) + bias_ref[h, :, :w]
        s_n = _dot_nt(qh, kn_ref[:, hs]) + bias_ref[h, :, w:]
        mx = jnp.maximum(jnp.max(s_c, axis=-1, keepdims=True), jnp.max(s_n, axis=-1, keepdims=True))
        e_c = jnp.exp(s_c - mx)
        e_n = jnp.exp(s_n - mx)
        den = jnp.sum(e_c, axis=-1, keepdims=True) + jnp.sum(e_n, axis=-1, keepdims=True)
        o = _dot(e_c.astype(BF16), vc_ref[rows_h, :].astype(BF16)) + _dot(e_n.astype(BF16), vn_ref[:, hs])
        o_ref[:, hs] = (o * (1.0 / den)).astype(BF16)


def _attn_s_call(q, k, v, cache_k, cache_v, toep, o_prompt, l):
    first = ROWS_P // DEC_SEQ
    new = pl.BlockSpec((DEC_SEQ, ATTN_DIM), lambda b: (first + b, 0))
    cache = pl.BlockSpec((None, None, ATTN_REACH * N_HEADS, HEAD_DIM), lambda b: (l, b, 0, 0))
    return _call(
        _attn_s_body, (DEC_BATCH,),
        [new, new, new, cache, cache, _resident((None, N_HEADS, TOEP), (l, 0, 0)),
         pl.BlockSpec(memory_space=pl.ANY)],
        new, jax.ShapeDtypeStruct((ROWS, ATTN_DIM), BF16),
        scratch=[pltpu.VMEM((N_HEADS, DEC_SEQ, ATTN_REACH + DEC_SEQ), F32)],
        aliases={6: 0}, name="attn_sample", vmem_mb=32)(q, k, v, cache_k, cache_v, toep, o_prompt)


def _toeplitz_table(rel_tab):
    left = 3 * QB - 1 - REL_CLIP
    right = TOEP - left - (2 * REL_CLIP + 1)
    pad = [(0, 0)] * (rel_tab.ndim - 1) + [(left, right)]
    return jnp.pad(rel_tab[..., ::-1], pad, mode="edge").astype(F32)


def kernel(x_prompt, x_sample, p_prompt, p_sample, cache_conv, cache_k, cache_v, norm_mix, w_in, conv_w, conv_b, conv_ln_g, conv_ln_b, w_a_out, gmlp_ln_g, gmlp_ln_b, gmlp_ws, gmlp_bs, w_b_out, attn_rel_bias, w_c_out, w_o, norm_ffn, w_ff1, w_ff2, norm_ple, w_ple_gate, w_ple_proj, norm_final):
    h = jnp.concatenate([x_prompt.reshape(ROWS_P, D_MODEL), x_sample.reshape(ROWS_S, D_MODEL)], axis=0)
    p_all = jnp.concatenate([p_prompt.reshape(DEPTH, ROWS_P, PLE_DIM),
                             p_sample.reshape(DEPTH, ROWS_S, PLE_DIM)], axis=1)
    cache_k = cache_k.reshape(DEPTH, DEC_BATCH, ATTN_REACH * N_HEADS, HEAD_DIM)
    cache_v = cache_v.reshape(DEPTH, DEC_BATCH, ATTN_REACH * N_HEADS, HEAD_DIM)
    toep = _toeplitz_table(attn_rel_bias)

    vec = lambda x: x.reshape(x.shape[0], 1, x.shape[-1])
    g_mix = vec(jnp.concatenate([norm_mix, norm_final[None]], axis=0))
    conv_b3, conv_g3, conv_lb3 = vec(conv_b), vec(conv_ln_g), vec(conv_ln_b)
    gm_g3, gm_b3 = vec(gmlp_ln_g), vec(gmlp_ln_b)
    g_ffn3, g_ple3 = vec(norm_ffn), vec(norm_ple)

    w_in_b, w_a_b, w_b_b, w_c_b, w_o_b = (x.astype(BF16) for x in (w_in, w_a_out, w_b_out, w_c_out, w_o))
    w_ff1_b, w_ff2_b, w_pg_b, w_pp_b = (x.astype(BF16) for x in (w_ff1, w_ff2, w_ple_gate, w_ple_proj))

    outs = [[] for _ in range(7)]
    xn = _rms_call(h, g_mix, 0)
    for l in range(DEPTH):
        a = _glu_call(xn, w_in_b, l)
        u, vg = _gmlp_in_call(xn, w_in_b, gm_g3, gm_b3, l)
        q, k, v, kv32 = _qkv_call(xn, w_in_b, l)
        gates = _gate_call(xn, w_in_b, l)

        ya_in = _conv_call(a, cache_conv, conv_w, conv_b3, conv_g3, conv_lb3, l)

        ws = gmlp_ws[l]
        corner = jnp.tile(ws[:, :DEC_SEQ, :DEC_SEQ], (1, GMLP_CHUNK // DEC_SEQ, GMLP_CHUNK // DEC_SEQ))
        bias_p = jnp.repeat(gmlp_bs[l].T, GMLP_GC, axis=1)
        bias_s = jnp.tile(bias_p[:DEC_SEQ], (GMLP_CHUNK // DEC_SEQ, 1))
        yb_in = _gmlp_mix_call(u, vg, jnp.stack([ws, corner]), jnp.stack([bias_p, bias_s]))

        o = _attn_p_call(q, k, v, toep, l)
        o = _attn_s_call(q, k, v, cache_k, cache_v, toep, o, l)

        h1, hn = _merge_call(ya_in, yb_in, o, gates, h, w_a_b, w_b_b, w_c_b, w_o_b, g_ffn3, l)
        h2 = _ffn_call(hn, h1, w_ff1_b, w_ff2_b, l)
        last = l == DEPTH - 1
        h, xn = _ple_call(h2, p_all, w_pg_b, w_pp_b, g_ple3, g_mix, l, l + 1, F32 if last else BF16)

        a_p = a[:ROWS_P].reshape(BATCH, SEQ, CONV_DIM)
        a_s = a[ROWS_P:].reshape(DEC_BATCH, DEC_SEQ, CONV_DIM)
        outs[0].append(a_p[:, SEQ - CONV_HIST:])
        outs[1].append(a_s[:, DEC_SEQ - CONV_HIST:])
        kv_p = kv32[:BATCH * TM].reshape(BATCH, TM, 2, N_HEADS, HEAD_DIM)
        kv_s = kv32[BATCH * TM:(BATCH + 1) * TM].reshape(DEC_BATCH, DEC_SEQ, 2, N_HEADS, HEAD_DIM)
        outs[2].append(kv_p[:, :, 0])
        outs[3].append(kv_p[:, :, 1])
        outs[4].append(kv_s[:, :, 0])
        outs[5].append(kv_s[:, :, 1])
        outs[6].append(vg[ROWS_P:].reshape(DEC_BATCH, DEC_SEQ, GMLP_DIM))

    y_prompt = xn[:ROWS_P].reshape(BATCH, SEQ, D_MODEL)
    y_sample = xn[ROWS_P:].reshape(DEC_BATCH, DEC_SEQ, D_MODEL)
    return (y_prompt, y_sample) + tuple(jnp.stack(x) for x in outs)
```

```python
import functools

import jax
import jax.numpy as jnp
from jax import lax
from jax.experimental import pallas as pl
from jax.experimental.pallas import tpu as pltpu

F32 = jnp.float32
BF16 = jnp.bfloat16

D_MODEL = 2048
BATCH = 2
SEQ = 4096
DEPTH = 4
DEC_BATCH = 16
DEC_SEQ = 32
CHUNK = 64
CONV_DIM = 1024
CONV_WIDTH = 31
CONV_HIST = CONV_WIDTH - 1
GMLP_DIM = 1024
GMLP_GROUPS = 8
GMLP_GC = GMLP_DIM // GMLP_GROUPS
GMLP_CHUNK = 128
N_HEADS = 8
HEAD_DIM = 128
ATTN_DIM = N_HEADS * HEAD_DIM
LEFT_CHUNKS = 8
ATTN_REACH = LEFT_CHUNKS * CHUNK
REL_CLIP = 128
D_FF = 4 * D_MODEL
PLE_DIM = 256
EPS = 1e-6
SCALE = HEAD_DIM ** -0.5
NEG = -1e30

ROWS_P = BATCH * SEQ
ROWS_S = DEC_BATCH * DEC_SEQ
ROWS = ROWS_P + ROWS_S
TM = 512
NT = ROWS // TM
NT_P = ROWS_P // TM
TILES_PER_SEQ = SEQ // TM
TM_MERGE = 256
TF = 1024
COLB = 1024
QB = 256
KB = 3 * QB
CONV_RC = 64
CONV_LB = 128
SUBLANES = 8
TOEP = 1024
KW = 8
KW_WIDE = 16
MIB = 1 << 20


def _rms(x, g):
    return x * lax.rsqrt(jnp.mean(x * x, axis=-1, keepdims=True) + EPS) * g


def _ln(x, g, b):
    mu = jnp.mean(x, axis=-1, keepdims=True)
    xc = x - mu
    var = jnp.mean(xc * xc, axis=-1, keepdims=True)
    return xc * lax.rsqrt(var + EPS) * g + b


def _dot(a, b):
    return jnp.dot(a, b, preferred_element_type=F32)


def _dot_nt(a, b):
    return lax.dot_general(a, b, (((1,), (1,)), ((), ())), preferred_element_type=F32)


def _call(body, grid, in_specs, out_specs, out_shape, *, name, vmem_mb, scratch=()):
    return pl.pallas_call(
        body,
        grid=grid,
        in_specs=in_specs,
        out_specs=out_specs,
        out_shape=out_shape,
        scratch_shapes=list(scratch),
        compiler_params=pltpu.CompilerParams(
            dimension_semantics=("arbitrary",) * len(grid), vmem_limit_bytes=vmem_mb * MIB),
        name=name,
    )


def _rows(cols, tm=TM, cb=0, lag=0):
    return pl.BlockSpec((tm, cols), lambda s, *_: (jnp.maximum(s - lag, 0), cb))


def _resident(block, index):
    return pl.BlockSpec(block, lambda *_: index, pipeline_mode=pl.Buffered(1))


def _vec(l, n):
    return _resident((None, 1, n), (l, 0, 0))


def _wchunk(l, k, cols, kw, cb=0):
    return pl.BlockSpec((None, k // kw, cols), lambda s, *_: (l, jnp.minimum(s, kw - 1), cb))


def _wcopy(k, cols):
    return pltpu.VMEM((k, cols), BF16)


def _stage_weights(chunk_refs, w_refs, kw):
    s = pl.program_id(0)

    @pl.when(s < kw)
    def _():
        for c_ref, w_ref in zip(chunk_refs, w_refs):
            rows = c_ref.shape[0]
            w_ref[pl.ds(pl.multiple_of(s * rows, rows), rows), :] = c_ref[...].astype(BF16)

    return s >= kw


def _rms_body(h_ref, g_ref, o_ref):
    o_ref[...] = _rms(h_ref[...], g_ref[...]).astype(o_ref.dtype)


def _rms_call(h, g3, l):
    return _call(
        _rms_body, (NT,), [_rows(D_MODEL), _vec(l, D_MODEL)], _rows(D_MODEL),
        jax.ShapeDtypeStruct((ROWS, D_MODEL), BF16), name="rms_first", vmem_mb=32)(h, g3)


def _glu_body(x_ref, ca_ref, cg_ref, a_ref, wa_ref, wg_ref):
    @pl.when(_stage_weights((ca_ref, cg_ref), (wa_ref, wg_ref), KW))
    def _():
        x = x_ref[...]
        a_ref[...] = _dot(x, wa_ref[...]) * jax.nn.sigmoid(_dot(x, wg_ref[...]))


def _glu_call(xn, w_in, l):
    return _call(
        _glu_body, (KW + NT,),
        [_rows(D_MODEL, lag=KW), _wchunk(l, D_MODEL, COLB, KW, 0), _wchunk(l, D_MODEL, COLB, KW, 1)],
        _rows(CONV_DIM, lag=KW), jax.ShapeDtypeStruct((ROWS, CONV_DIM), F32),
        scratch=[_wcopy(D_MODEL, COLB)] * 2,
        name="inproj_glu", vmem_mb=44)(xn, w_in, w_in)


def _gmlp_in_body(x_ref, cu_ref, cv_ref, g_ref, b_ref, u_ref, v_ref, wu_ref, wv_ref):
    @pl.when(_stage_weights((cu_ref, cv_ref), (wu_ref, wv_ref), KW))
    def _():
        x = x_ref[...]
        v_ref[...] = _ln(jax.nn.gelu(_dot(x, wv_ref[...])), g_ref[...], b_ref[...])
        u_ref[...] = jax.nn.gelu(_dot(x, wu_ref[...])).astype(BF16)


def _gmlp_in_call(xn, w_in, ln_g, ln_b, l):
    return _call(
        _gmlp_in_body, (KW + NT,),
        [_rows(D_MODEL, lag=KW), _wchunk(l, D_MODEL, COLB, KW, 2), _wchunk(l, D_MODEL, COLB, KW, 3),
         _vec(l, GMLP_DIM), _vec(l, GMLP_DIM)],
        [_rows(GMLP_DIM, lag=KW), _rows(GMLP_DIM, lag=KW)],
        [jax.ShapeDtypeStruct((ROWS, GMLP_DIM), BF16), jax.ShapeDtypeStruct((ROWS, GMLP_DIM), F32)],
        scratch=[_wcopy(D_MODEL, COLB)] * 2,
        name="inproj_gmlp", vmem_mb=44)(xn, w_in, w_in, ln_g, ln_b)


KV_SLOTS = BATCH + 1


def _qkv_body(x_ref, cq_ref, ck_ref, cv_ref, q_ref, k_ref, v_ref, kv32_ref, wq_ref, wk_ref, wv_ref):
    @pl.when(_stage_weights((cq_ref, ck_ref, cv_ref), (wq_ref, wk_ref, wv_ref), KW))
    def _():
        x = x_ref[...]
        q_ref[...] = (_dot(x, wq_ref[...]) * SCALE).astype(BF16)
        k = _dot(x, wk_ref[...])
        k_ref[...] = k.astype(BF16)
        kv32_ref[:, :ATTN_DIM] = k
        v = _dot(x, wv_ref[...])
        v_ref[...] = v.astype(BF16)
        kv32_ref[:, ATTN_DIM:] = v


def _qkv_call(xn, w_in, l):
    act = jax.ShapeDtypeStruct((ROWS, ATTN_DIM), BF16)
    kv_slot = lambda s: (jnp.maximum(s - KW, 0) // TILES_PER_SEQ, 0)
    return _call(
        _qkv_body, (KW + NT,),
        [_rows(D_MODEL, lag=KW)] + [_wchunk(l, D_MODEL, COLB, KW, 4 + j) for j in range(3)],
        [_rows(ATTN_DIM, lag=KW)] * 3 + [pl.BlockSpec((TM, 2 * ATTN_DIM), kv_slot)],
        [act, act, act, jax.ShapeDtypeStruct((KV_SLOTS * TM, 2 * ATTN_DIM), F32)],
        scratch=[_wcopy(D_MODEL, COLB)] * 3,
        name="inproj_qkv", vmem_mb=52)(xn, w_in, w_in, w_in)


N_GATE_BLOCKS = 3 * D_MODEL // COLB


def _gate_body(x_ref, *refs):
    n = N_GATE_BLOCKS
    c_refs, o_ref, w_refs = refs[:n], refs[n], refs[n + 1:]

    @pl.when(_stage_weights(c_refs, w_refs, KW_WIDE))
    def _():
        x = x_ref[...]
        for j, w_ref in enumerate(w_refs):
            o_ref[:, j * COLB:(j + 1) * COLB] = jax.nn.sigmoid(_dot(x, w_ref[...])).astype(BF16)


def _gate_call(xn, w_in, l):
    n = N_GATE_BLOCKS
    return _call(
        _gate_body, (KW_WIDE + NT,),
        [_rows(D_MODEL, lag=KW_WIDE)] + [_wchunk(l, D_MODEL, COLB, KW_WIDE, 7 + j) for j in range(n)],
        _rows(3 * D_MODEL, lag=KW_WIDE), jax.ShapeDtypeStruct((ROWS, 3 * D_MODEL), BF16),
        scratch=[_wcopy(D_MODEL, COLB)] * n,
        name="inproj_gates", vmem_mb=56)(xn, *([w_in] * n))


def _merge_body(a_ref, b_ref, cp_ref, cs_ref, g0_ref, g1_ref, g2_ref, h_ref,
                ca_ref, cb_ref, cc_ref, co_ref, gf_ref, h1_ref, hn_ref,
                wa_ref, wb_ref, wc_ref, wo_ref):
    main = _stage_weights((ca_ref, cb_ref, cc_ref, co_ref), (wa_ref, wb_ref, wc_ref, wo_ref), KW_WIDE)

    @pl.when(main)
    def _():
        is_sample = pl.program_id(0) - KW_WIDE >= ROWS_P // TM_MERGE
        c = jnp.where(is_sample, cs_ref[...], cp_ref[...])
        m = g0_ref[...].astype(F32) * _dot(a_ref[...], wa_ref[...])
        m = m + g1_ref[...].astype(F32) * _dot(b_ref[...], wb_ref[...])
        m = m + g2_ref[...].astype(F32) * _dot(c, wc_ref[...])
        h1 = h_ref[...] + _dot(m.astype(BF16), wo_ref[...])
        h1_ref[...] = h1
        hn_ref[...] = _rms(h1, gf_ref[...]).astype(BF16)


def _merge_call(a, b, c_p, c_s, gates, h, w_a, w_b, w_c, w_o, g_ffn, l):
    tm, kw = TM_MERGE, KW_WIDE
    np_ = ROWS_P // tm
    rows = functools.partial(_rows, tm=tm, lag=kw)
    return _call(
        _merge_body, (kw + ROWS // tm,),
        [rows(CONV_DIM), rows(GMLP_DIM),
         pl.BlockSpec((tm, ATTN_DIM), lambda s: (jnp.clip(s - kw, 0, np_ - 1), 0)),
         pl.BlockSpec((tm, ATTN_DIM), lambda s: (jnp.maximum(s - kw - np_, 0), 0)),
         rows(D_MODEL, cb=0), rows(D_MODEL, cb=1), rows(D_MODEL, cb=2), rows(D_MODEL),
         _wchunk(l, CONV_DIM, D_MODEL, kw), _wchunk(l, GMLP_DIM, D_MODEL, kw),
         _wchunk(l, ATTN_DIM, D_MODEL, kw), _wchunk(l, D_MODEL, D_MODEL, kw), _vec(l, D_MODEL)],
        [rows(D_MODEL), rows(D_MODEL)],
        [jax.ShapeDtypeStruct((ROWS, D_MODEL), F32), jax.ShapeDtypeStruct((ROWS, D_MODEL), BF16)],
        scratch=[_wcopy(CONV_DIM, D_MODEL), _wcopy(GMLP_DIM, D_MODEL), _wcopy(ATTN_DIM, D_MODEL),
                 _wcopy(D_MODEL, D_MODEL)],
        name="merge_out", vmem_mb=58)(a, b, c_p, c_s, gates, gates, gates, h, w_a, w_b, w_c, w_o, g_ffn)


def _ffn_body(hn_ref, w1_ref, w2_ref, h1_ref, o_ref):
    @pl.when(pl.program_id(1) == 0)
    def _():
        o_ref[...] = h1_ref[...]

    a = jnp.square(jnp.maximum(_dot(hn_ref[...], w1_ref[...]), 0.0)).astype(BF16)
    o_ref[...] += _dot(a, w2_ref[...])


def _ffn_call(hn, h1, w1, w2, l):
    return _call(
        _ffn_body, (NT, D_FF // TF),
        [pl.BlockSpec((TM, D_MODEL), lambda i, f: (i, 0)),
         pl.BlockSpec((None, D_MODEL, TF), lambda i, f: (l, 0, f)),
         pl.BlockSpec((None, TF, D_MODEL), lambda i, f: (l, f, 0)),
         pl.BlockSpec((TM, D_MODEL), lambda i, f: (i, 0))],
        pl.BlockSpec((TM, D_MODEL), lambda i, f: (i, 0)),
        jax.ShapeDtypeStruct((ROWS, D_MODEL), F32),
        name="ffn", vmem_mb=56)(hn, w1, w2, h1)


def _ple_update(h2_ref, p_ref, wg_ref, wp_ref, gp_ref):
    h2 = h2_ref[...]
    gate = jax.nn.sigmoid(_dot(_rms(h2, gp_ref[...]).astype(BF16), wg_ref[...]))
    return h2 + gate * _dot(p_ref[...].astype(BF16), wp_ref[...])


def _ple_body(h2_ref, p_ref, cg_ref, cp_ref, gp_ref, gn_ref, h3_ref, xn_ref, wg_ref, wp_ref):
    @pl.when(_stage_weights((cg_ref, cp_ref), (wg_ref, wp_ref), KW))
    def _():
        h3 = _ple_update(h2_ref, p_ref, wg_ref, wp_ref, gp_ref)
        h3_ref[...] = h3
        xn_ref[...] = _rms(h3, gn_ref[...]).astype(BF16)


def _ple_last_body(h2_ref, p_ref, cg_ref, cp_ref, gp_ref, gn_ref, yp_ref, ys_ref, wg_ref, wp_ref):
    main = _stage_weights((cg_ref, cp_ref), (wg_ref, wp_ref), KW)
    tile = pl.program_id(0) - KW

    def final():
        return _rms(_ple_update(h2_ref, p_ref, wg_ref, wp_ref, gp_ref), gn_ref[...])

    @pl.when(main & (tile < NT_P))
    def _():
        yp_ref[...] = final()

    @pl.when(tile >= NT_P)
    def _():
        ys_ref[...] = final()


def _ple_call(h2, p_all, w_g, w_p, g_ple, g_next, l, last):
    ins = [_rows(D_MODEL, lag=KW), pl.BlockSpec((None, TM, PLE_DIM), lambda s: (l, jnp.maximum(s - KW, 0), 0)),
           _wchunk(l, D_MODEL, D_MODEL, KW), _wchunk(l, PLE_DIM, D_MODEL, KW),
           _vec(l, D_MODEL), _vec(l + 1, D_MODEL)]
    scratch = [_wcopy(D_MODEL, D_MODEL), _wcopy(PLE_DIM, D_MODEL)]
    args = (h2, p_all, w_g, w_p, g_ple, g_next)
    if not last:
        return _call(
            _ple_body, (KW + NT,), ins, [_rows(D_MODEL, lag=KW)] * 2,
            [jax.ShapeDtypeStruct((ROWS, D_MODEL), F32), jax.ShapeDtypeStruct((ROWS, D_MODEL), BF16)],
            scratch=scratch, name="ple", vmem_mb=56)(*args)
    assert ROWS_S == TM
    return _call(
        _ple_last_body, (KW + NT,), ins,
        [pl.BlockSpec((TM, D_MODEL), lambda s: (jnp.clip(s - KW, 0, NT_P - 1), 0)),
         pl.BlockSpec((TM, D_MODEL), lambda s: (0, 0))],
        [jax.ShapeDtypeStruct((ROWS_P, D_MODEL), F32), jax.ShapeDtypeStruct((ROWS_S, D_MODEL), F32)],
        scratch=scratch, name="ple_final", vmem_mb=56)(*args)


def _conv_rows(ext_ref, y_ref, cw_ref, ext_base, y_base, rc):
    for lb in range(CONV_DIM // CONV_LB):
        ls = slice(lb * CONV_LB, (lb + 1) * CONV_LB)
        acc = None
        for r in range(SUBLANES):
            rows = rc + (SUBLANES if r else 0)
            u = None
            for j in range(CONV_WIDTH):
                if (2 + j) % SUBLANES == r:
                    start = ext_base + (2 + j - r)
                    if not isinstance(start, int):
                        start = pl.multiple_of(start, SUBLANES)
                    term = cw_ref[j:j + 1, ls] * ext_ref[pl.ds(start, rows), ls]
                    u = term if u is None else u + term
            part = u[r:r + rc, :]
            acc = part if acc is None else acc + part
        y_ref[pl.ds(y_base, rc), ls] = acc


def _conv_body(a_ref, cache_ref, cw_ref, cb_ref, g_ref, b_ref, o_ref, ext_ref, y_ref):
    i = pl.program_id(0)
    hist = 32

    @pl.when(i < NT_P)
    def _():
        @pl.when(i % TILES_PER_SEQ == 0)
        def _():
            ext_ref[0:hist, :] = jnp.zeros((hist, CONV_DIM), F32)

        @pl.when(i % TILES_PER_SEQ != 0)
        def _():
            ext_ref[0:hist, :] = ext_ref[TM:TM + hist, :]

        ext_ref[hist:hist + TM, :] = a_ref[...]

        def chunk(c, carry):
            r0 = pl.multiple_of(c * CONV_RC, CONV_RC)
            _conv_rows(ext_ref, y_ref, cw_ref, r0, r0, CONV_RC)
            return carry

        lax.fori_loop(0, TM // CONV_RC, chunk, 0)

    @pl.when(i == NT_P)
    def _():
        def seq(s, carry):
            r0 = pl.multiple_of(s * DEC_SEQ, DEC_SEQ)
            ext_ref[hist - CONV_HIST:hist, :] = cache_ref[s]
            ext_ref[hist:hist + DEC_SEQ, :] = a_ref[pl.ds(r0, DEC_SEQ), :]
            _conv_rows(ext_ref, y_ref, cw_ref, 0, r0, DEC_SEQ)
            return carry

        lax.fori_loop(0, DEC_BATCH, seq, 0)

    y = _ln(y_ref[...] + cb_ref[...], g_ref[...], b_ref[...])
    o_ref[...] = (y * jax.nn.sigmoid(y)).astype(BF16)


def _conv_call(a, cache_conv, conv_w, conv_b, ln_g, ln_b, l):
    return _call(
        _conv_body, (NT,),
        [_rows(CONV_DIM),
         _resident((None, DEC_BATCH, CONV_HIST, CONV_DIM), (l, 0, 0, 0)),
         _resident((None, CONV_WIDTH, CONV_DIM), (l, 0, 0)),
         _vec(l, CONV_DIM), _vec(l, CONV_DIM), _vec(l, CONV_DIM)],
        _rows(CONV_DIM), jax.ShapeDtypeStruct((ROWS, CONV_DIM), BF16),
        scratch=[pltpu.VMEM((TM + 32, CONV_DIM), F32), pltpu.VMEM((TM, CONV_DIM), F32)],
        name="conv_mixer", vmem_mb=32)(a, cache_conv, conv_w, conv_b, ln_g, ln_b)


def _gmlp_mix_body(u_ref, v_ref, w_ref, bias_ref, o_ref):
    shift = jnp.where(pl.program_id(0) == NT_P, 5, 7)
    r = lax.broadcasted_iota(jnp.int32, (GMLP_CHUNK, GMLP_CHUNK), 0)
    c = lax.broadcasted_iota(jnp.int32, (GMLP_CHUNK, GMLP_CHUNK), 1)
    mask = (r >= c) & (lax.shift_right_logical(r, shift) == lax.shift_right_logical(c, shift))
    for g in range(GMLP_GROUPS):
        cs = slice(g * GMLP_GC, (g + 1) * GMLP_GC)
        wg = jnp.where(mask, w_ref[g], 0.0).astype(BF16)
        bias = bias_ref[:, cs]
        for k in range(TM // GMLP_CHUNK):
            rs = slice(k * GMLP_CHUNK, (k + 1) * GMLP_CHUNK)
            mixed = _dot(wg, v_ref[rs, cs].astype(BF16)) + bias
            o_ref[rs, cs] = (u_ref[rs, cs].astype(F32) * mixed).astype(BF16)


def _gmlp_mix_call(u, v, wmix, bmix):
    pick = lambda i: (jnp.where(i == NT_P, 1, 0), 0, 0, 0)
    return _call(
        _gmlp_mix_body, (NT,),
        [_rows(GMLP_DIM), _rows(GMLP_DIM),
         pl.BlockSpec((None, GMLP_GROUPS, GMLP_CHUNK, GMLP_CHUNK), pick),
         pl.BlockSpec((None, GMLP_CHUNK, GMLP_DIM), lambda i: pick(i)[:3])],
        _rows(GMLP_DIM), jax.ShapeDtypeStruct((ROWS, GMLP_DIM), BF16),
        name="gmlp_mix", vmem_mb=32)(u, v, wmix, bmix)


def _toeplitz_rows(tab_ref, h, rows):
    t = jnp.broadcast_to(tab_ref[h:h + 1, :], (rows, TOEP))
    return pltpu.roll(t, TOEP - QB + 1, 1, stride=1, stride_axis=0)


def _attn_p_body(q_ref, k0_ref, k1_ref, k2_ref, v0_ref, v1_ref, v2_ref, tab_ref, o_ref, bias_ref):
    m = pl.program_id(1)
    k_refs = (k0_ref, k1_ref, k2_ref)
    v_refs = (v0_ref, v1_ref, v2_ref)

    @pl.when((pl.program_id(0) == 0) & (m == 0))
    def _():
        r = lax.broadcasted_iota(jnp.int32, (QB, KB), 0) // CHUNK
        j = lax.broadcasted_iota(jnp.int32, (QB, KB), 1) // CHUNK
        band = (j >= r) & (j <= r + LEFT_CHUNKS)
        for h in range(N_HEADS):
            bias = jnp.where(band, _toeplitz_rows(tab_ref, h, QB)[:, :KB], NEG)
            for s in range(3):
                bias_ref[h, s] = bias[:, s * QB:(s + 1) * QB]
            bias_ref[h, 3] = jnp.full((QB, QB), NEG, F32)

    slot = [jnp.where(m - 2 + s >= 0, s, 3) for s in range(3)]
    for h in range(N_HEADS):
        hs = slice(h * HEAD_DIM, (h + 1) * HEAD_DIM)
        qh = q_ref[:, hs]
        sc = [_dot_nt(qh, k_refs[s][:, hs]) + bias_ref[h, slot[s]] for s in range(3)]
        mx = jnp.max(functools.reduce(jnp.maximum, sc), axis=-1, keepdims=True)
        e = [jnp.exp(x - mx) for x in sc]
        den = jnp.sum(functools.reduce(jnp.add, e), axis=-1, keepdims=True)
        o = functools.reduce(jnp.add, [_dot(e[s].astype(BF16), v_refs[s][:, hs]) for s in range(3)])
        o_ref[:, hs] = (o * (1.0 / den)).astype(BF16)


def _attn_p_call(q, k, v, toep, l):
    nq = SEQ // QB

    def kv_spec(s):
        return pl.BlockSpec((QB, ATTN_DIM), lambda b, m: (b * nq + jnp.maximum(m - 2 + s, 0), 0))

    return _call(
        _attn_p_body, (BATCH, nq),
        [pl.BlockSpec((QB, ATTN_DIM), lambda b, m: (b * nq + m, 0))]
        + [kv_spec(s) for s in range(3)] + [kv_spec(s) for s in range(3)]
        + [_resident((None, N_HEADS, TOEP), (l, 0, 0))],
        pl.BlockSpec((QB, ATTN_DIM), lambda b, m: (b * nq + m, 0)),
        jax.ShapeDtypeStruct((ROWS_P, ATTN_DIM), BF16),
        scratch=[pltpu.VMEM((N_HEADS, 4, QB, QB), F32)],
        name="attn_prompt", vmem_mb=40)(q, k, k, k, v, v, v, toep)


def _attn_s_body(q_ref, kn_ref, vn_ref, kc_ref, vc_ref, tab_ref, o_ref, bias_ref):
    w = ATTN_REACH

    @pl.when(pl.program_id(0) == 0)
    def _():
        for h in range(N_HEADS):
            bias_ref[h] = _toeplitz_rows(tab_ref, h, DEC_SEQ)[:, :w + DEC_SEQ]

    for h in range(N_HEADS):
        hs = slice(h * HEAD_DIM, (h + 1) * HEAD_DIM)
        rows_h = pl.ds(h, w, stride=N_HEADS)
        qh = q_ref[:, hs]
        s_c = _dot_nt(qh, kc_ref[rows_h, :].astype(BF16)) + bias_ref[h, :, :w]
        s_n = _dot_nt(qh, kn_ref[:, hs]) + bias_ref[h, :, w:]
        mx = jnp.maximum(jnp.max(s_c, axis=-1, keepdims=True), jnp.max(s_n, axis=-1, keepdims=True))
        e_c = jnp.exp(s_c - mx)
        e_n = jnp.exp(s_n - mx)
        den = jnp.sum(e_c, axis=-1, keepdims=True) + jnp.sum(e_n, axis=-1, keepdims=True)
        o = _dot(e_c.astype(BF16), vc_ref[rows_h, :].astype(BF16)) + _dot(e_n.astype(BF16), vn_ref[:, hs])
        o_ref[:, hs] = (o * (1.0 / den)).astype(BF16)


def _attn_s_call(q, k, v, cache_k, cache_v, toep, l):
    first = ROWS_P // DEC_SEQ
    new = pl.BlockSpec((DEC_SEQ, ATTN_DIM), lambda b: (first + b, 0))
    cache = pl.BlockSpec((None, None, ATTN_REACH * N_HEADS, HEAD_DIM), lambda b: (l, b, 0, 0))
    return _call(
        _attn_s_body, (DEC_BATCH,),
        [new, new, new, cache, cache, _resident((None, N_HEADS, TOEP), (l, 0, 0))],
        pl.BlockSpec((DEC_SEQ, ATTN_DIM), lambda b: (b, 0)),
        jax.ShapeDtypeStruct((ROWS_S, ATTN_DIM), BF16),
        scratch=[pltpu.VMEM((N_HEADS, DEC_SEQ, ATTN_REACH + DEC_SEQ), F32)],
        name="attn_sample", vmem_mb=32)(q, k, v, cache_k, cache_v, toep)


def _toeplitz_table(rel_tab):
    left = 3 * QB - 1 - REL_CLIP
    right = TOEP - left - (2 * REL_CLIP + 1)
    pad = [(0, 0)] * (rel_tab.ndim - 1) + [(left, right)]
    return jnp.pad(rel_tab[..., ::-1], pad, mode="edge").astype(F32)


def kernel(x_prompt, x_sample, p_prompt, p_sample, cache_conv, cache_k, cache_v, norm_mix, w_in, conv_w, conv_b, conv_ln_g, conv_ln_b, w_a_out, gmlp_ln_g, gmlp_ln_b, gmlp_ws, gmlp_bs, w_b_out, attn_rel_bias, w_c_out, w_o, norm_ffn, w_ff1, w_ff2, norm_ple, w_ple_gate, w_ple_proj, norm_final):
    h = jnp.concatenate([x_prompt.reshape(ROWS_P, D_MODEL), x_sample.reshape(ROWS_S, D_MODEL)], axis=0)
    p_all = jnp.concatenate([p_prompt.reshape(DEPTH, ROWS_P, PLE_DIM),
                             p_sample.reshape(DEPTH, ROWS_S, PLE_DIM)], axis=1)
    cache_k = cache_k.reshape(DEPTH, DEC_BATCH, ATTN_REACH * N_HEADS, HEAD_DIM)
    cache_v = cache_v.reshape(DEPTH, DEC_BATCH, ATTN_REACH * N_HEADS, HEAD_DIM)
    toep = _toeplitz_table(attn_rel_bias)

    vec = lambda x: x.reshape(x.shape[0], 1, x.shape[-1])
    g_mix = vec(jnp.concatenate([norm_mix, norm_final[None]], axis=0))
    conv_b3, conv_g3, conv_lb3 = vec(conv_b), vec(conv_ln_g), vec(conv_ln_b)
    gm_g3, gm_b3 = vec(gmlp_ln_g), vec(gmlp_ln_b)
    g_ffn3, g_ple3 = vec(norm_ffn), vec(norm_ple)

    w_ff1_b, w_ff2_b = w_ff1.astype(BF16), w_ff2.astype(BF16)

    outs = [[] for _ in range(7)]
    xn = _rms_call(h, g_mix, 0)
    for l in range(DEPTH):
        a = _glu_call(xn, w_in, l)
        u, vg = _gmlp_in_call(xn, w_in, gm_g3, gm_b3, l)
        q, k, v, kv32 = _qkv_call(xn, w_in, l)
        gates = _gate_call(xn, w_in, l)

        ya_in = _conv_call(a, cache_conv, conv_w, conv_b3, conv_g3, conv_lb3, l)

        ws = gmlp_ws[l]
        corner = jnp.tile(ws[:, :DEC_SEQ, :DEC_SEQ], (1, GMLP_CHUNK // DEC_SEQ, GMLP_CHUNK // DEC_SEQ))
        bias_p = jnp.repeat(gmlp_bs[l].T, GMLP_GC, axis=1)
        bias_s = jnp.tile(bias_p[:DEC_SEQ], (GMLP_CHUNK // DEC_SEQ, 1))
        yb_in = _gmlp_mix_call(u, vg, jnp.stack([ws, corner]), jnp.stack([bias_p, bias_s]))

        o_p = _attn_p_call(q, k, v, toep, l)
        o_s = _attn_s_call(q, k, v, cache_k, cache_v, toep, l)

        h1, hn = _merge_call(ya_in, yb_in, o_p, o_s, gates, h, w_a_out, w_b_out, w_c_out, w_o, g_ffn3, l)
        h2 = _ffn_call(hn, h1, w_ff1_b, w_ff2_b, l)
        last = l == DEPTH - 1
        h, xn = _ple_call(h2, p_all, w_ple_gate, w_ple_proj, g_ple3, g_mix, l, last)

        a_p = a[:ROWS_P].reshape(BATCH, SEQ, CONV_DIM)
        a_s = a[ROWS_P:].reshape(DEC_BATCH, DEC_SEQ, CONV_DIM)
        outs[0].append(a_p[:, SEQ - CONV_HIST:])
        outs[1].append(a_s[:, DEC_SEQ - CONV_HIST:])
        kv_p = kv32[:BATCH * TM].reshape(BATCH, TM, 2, N_HEADS, HEAD_DIM)
        kv_s = kv32[BATCH * TM:].reshape(DEC_BATCH, DEC_SEQ, 2, N_HEADS, HEAD_DIM)
        outs[2].append(kv_p[:, :, 0])
        outs[3].append(kv_p[:, :, 1])
        outs[4].append(kv_s[:, :, 0])
        outs[5].append(kv_s[:, :, 1])
        outs[6].append(vg[ROWS_P:].reshape(DEC_BATCH, DEC_SEQ, GMLP_DIM))

    y_prompt = h.reshape(BATCH, SEQ, D_MODEL)
    y_sample = xn.reshape(DEC_BATCH, DEC_SEQ, D_MODEL)
    return (y_prompt, y_sample) + tuple(jnp.stack(x) for x in outs)
```

```python
import functools

import jax
import jax.numpy as jnp
from jax import lax
from jax.experimental import pallas as pl
from jax.experimental.pallas import tpu as pltpu

F32 = jnp.float32
BF16 = jnp.bfloat16

D_MODEL = 2048
BATCH = 2
SEQ = 4096
DEPTH = 4
DEC_BATCH = 16
DEC_SEQ = 32
CHUNK = 64
CONV_DIM = 1024
CONV_WIDTH = 31
CONV_HIST = CONV_WIDTH - 1
GMLP_DIM = 1024
GMLP_GROUPS = 8
GMLP_GC = GMLP_DIM // GMLP_GROUPS
GMLP_CHUNK = 128
N_HEADS = 8
HEAD_DIM = 128
ATTN_DIM = N_HEADS * HEAD_DIM
LEFT_CHUNKS = 8
ATTN_REACH = LEFT_CHUNKS * CHUNK
REL_CLIP = 128
D_FF = 4 * D_MODEL
PLE_DIM = 256
EPS = 1e-6
SCALE = HEAD_DIM ** -0.5
NEG = -1e30

ROWS_P = BATCH * SEQ
ROWS_S = DEC_BATCH * DEC_SEQ
ROWS = ROWS_P + ROWS_S
TM = 512
NT = ROWS // TM
NT_P = ROWS_P // TM
TILES_PER_SEQ = SEQ // TM
TM_MERGE = 256
TF = 1024
COLB = 1024
QB = 256
KB = 3 * QB
CONV_RC = 64
CONV_LB = 128
SUBLANES = 8
TOEP = 1024
KW = 8
KW_WIDE = 16
SEQS_PER_STEP = 2
MIB = 1 << 20


def _rms(x, g):
    return x * lax.rsqrt(jnp.mean(x * x, axis=-1, keepdims=True) + EPS) * g


def _ln(x, g, b):
    mu = jnp.mean(x, axis=-1, keepdims=True)
    xc = x - mu
    var = jnp.mean(xc * xc, axis=-1, keepdims=True)
    return xc * lax.rsqrt(var + EPS) * g + b


def _dot(a, b):
    return jnp.dot(a, b, preferred_element_type=F32)


def _dot_nt(a, b):
    return lax.dot_general(a, b, (((1,), (1,)), ((), ())), preferred_element_type=F32)


def _call(body, grid, in_specs, out_specs, out_shape, *, name, vmem_mb, scratch=()):
    return pl.pallas_call(
        body,
        grid=grid,
        in_specs=in_specs,
        out_specs=out_specs,
        out_shape=out_shape,
        scratch_shapes=list(scratch),
        compiler_params=pltpu.CompilerParams(
            dimension_semantics=("arbitrary",) * len(grid), vmem_limit_bytes=vmem_mb * MIB),
        name=name,
    )


def _rows(cols, tm=TM, cb=0, lag=0):
    return pl.BlockSpec((tm, cols), lambda s, *_: (jnp.maximum(s - lag, 0), cb))


def _resident(block, index):
    return pl.BlockSpec(block, lambda *_: index, pipeline_mode=pl.Buffered(1))


def _vec(l, n):
    return _resident((None, 1, n), (l, 0, 0))


def _wchunk(l, k, cols, kw, cb=0):
    return pl.BlockSpec((None, k // kw, cols), lambda s, *_: (l, jnp.minimum(s, kw - 1), cb))


def _wcopy(k, cols):
    return pltpu.VMEM((k, cols), BF16)


def _stage_weights(chunk_refs, w_refs, kw):
    s = pl.program_id(0)

    @pl.when(s < kw)
    def _():
        for c_ref, w_ref in zip(chunk_refs, w_refs):
            rows = c_ref.shape[0]
            w_ref[pl.ds(pl.multiple_of(s * rows, rows), rows), :] = c_ref[...].astype(BF16)

    return s >= kw


def _group_specs(cols):
    assert ROWS_S == TM
    return [pl.BlockSpec((TM, cols), lambda i: (jnp.minimum(i, NT_P - 1), 0)),
            pl.BlockSpec((TM, cols), lambda i: (0, 0))]


def _first_body(xp_ref, xs_ref, g_ref, h_ref, xn_ref):
    x = jnp.where(pl.program_id(0) == NT_P, xs_ref[...], xp_ref[...])
    h_ref[...] = x
    xn_ref[...] = _rms(x, g_ref[...]).astype(BF16)


def _first_call(xp, xs, g3):
    return _call(
        _first_body, (NT,), _group_specs(D_MODEL) + [_vec(0, D_MODEL)], [_rows(D_MODEL)] * 2,
        [jax.ShapeDtypeStruct((ROWS, D_MODEL), F32), jax.ShapeDtypeStruct((ROWS, D_MODEL), BF16)],
        name="rms_first", vmem_mb=40)(xp, xs, g3)


def _glu_body(x_ref, ca_ref, cg_ref, a_ref, wa_ref, wg_ref):
    @pl.when(_stage_weights((ca_ref, cg_ref), (wa_ref, wg_ref), KW))
    def _():
        x = x_ref[...]
        a_ref[...] = _dot(x, wa_ref[...]) * jax.nn.sigmoid(_dot(x, wg_ref[...]))


def _glu_call(xn, w_in, l):
    return _call(
        _glu_body, (KW + NT,),
        [_rows(D_MODEL, lag=KW), _wchunk(l, D_MODEL, COLB, KW, 0), _wchunk(l, D_MODEL, COLB, KW, 1)],
        _rows(CONV_DIM, lag=KW), jax.ShapeDtypeStruct((ROWS, CONV_DIM), F32),
        scratch=[_wcopy(D_MODEL, COLB)] * 2,
        name="inproj_glu", vmem_mb=44)(xn, w_in, w_in)


def _gmlp_body(x_ref, cu_ref, cv_ref, g_ref, b_ref, ws_ref, bias_ref, yb_ref, vs_ref, wu_ref, wv_ref):
    main = _stage_weights((cu_ref, cv_ref), (wu_ref, wv_ref), KW)

    @pl.when(main)
    def _():
        x = x_ref[...]
        v = _ln(jax.nn.gelu(_dot(x, wv_ref[...])), g_ref[...], b_ref[...])
        vs_ref[...] = v
        u = jax.nn.gelu(_dot(x, wu_ref[...]))
        shift = jnp.where(pl.program_id(0) - KW == NT_P, 5, 7)
        r = lax.broadcasted_iota(jnp.int32, (GMLP_CHUNK, GMLP_CHUNK), 0)
        c = lax.broadcasted_iota(jnp.int32, (GMLP_CHUNK, GMLP_CHUNK), 1)
        mask = (r >= c) & (lax.shift_right_logical(r, shift) == lax.shift_right_logical(c, shift))
        for g in range(GMLP_GROUPS):
            cs = slice(g * GMLP_GC, (g + 1) * GMLP_GC)
            wg = jnp.where(mask, ws_ref[g], 0.0).astype(BF16)
            bias = bias_ref[:, cs]
            for k in range(TM // GMLP_CHUNK):
                rs = slice(k * GMLP_CHUNK, (k + 1) * GMLP_CHUNK)
                mixed = _dot(wg, v[rs, cs].astype(BF16)) + bias
                yb_ref[rs, cs] = (u[rs, cs] * mixed).astype(BF16)


def _gmlp_call(xn, w_in, ln_g, ln_b, wmix, bmix, l):
    assert ROWS_S == TM
    pick = lambda s: (jnp.where(s - KW == NT_P, 1, 0), 0, 0, 0)
    return _call(
        _gmlp_body, (KW + NT,),
        [_rows(D_MODEL, lag=KW), _wchunk(l, D_MODEL, COLB, KW, 2), _wchunk(l, D_MODEL, COLB, KW, 3),
         _vec(l, GMLP_DIM), _vec(l, GMLP_DIM),
         pl.BlockSpec((None, GMLP_GROUPS, GMLP_CHUNK, GMLP_CHUNK), pick),
         pl.BlockSpec((None, GMLP_CHUNK, GMLP_DIM), lambda s: pick(s)[:3])],
        [_rows(GMLP_DIM, lag=KW), pl.BlockSpec((TM, GMLP_DIM), lambda s: (0, 0))],
        [jax.ShapeDtypeStruct((ROWS, GMLP_DIM), BF16), jax.ShapeDtypeStruct((ROWS_S, GMLP_DIM), F32)],
        scratch=[_wcopy(D_MODEL, COLB)] * 2,
        name="gmlp_branch", vmem_mb=52)(xn, w_in, w_in, ln_g, ln_b, wmix, bmix)


KV_SLOTS = BATCH + 1


def _qkv_body(x_ref, cq_ref, ck_ref, cv_ref, q_ref, k_ref, v_ref, kv32_ref, wq_ref, wk_ref, wv_ref):
    @pl.when(_stage_weights((cq_ref, ck_ref, cv_ref), (wq_ref, wk_ref, wv_ref), KW))
    def _():
        x = x_ref[...]
        q_ref[...] = (_dot(x, wq_ref[...]) * SCALE).astype(BF16)
        k = _dot(x, wk_ref[...])
        k_ref[...] = k.astype(BF16)
        kv32_ref[:, :ATTN_DIM] = k
        v = _dot(x, wv_ref[...])
        v_ref[...] = v.astype(BF16)
        kv32_ref[:, ATTN_DIM:] = v


def _qkv_call(xn, w_in, l):
    act = jax.ShapeDtypeStruct((ROWS, ATTN_DIM), BF16)
    kv_slot = lambda s: (jnp.maximum(s - KW, 0) // TILES_PER_SEQ, 0)
    return _call(
        _qkv_body, (KW + NT,),
        [_rows(D_MODEL, lag=KW)] + [_wchunk(l, D_MODEL, COLB, KW, 4 + j) for j in range(3)],
        [_rows(ATTN_DIM, lag=KW)] * 3 + [pl.BlockSpec((TM, 2 * ATTN_DIM), kv_slot)],
        [act, act, act, jax.ShapeDtypeStruct((KV_SLOTS * TM, 2 * ATTN_DIM), F32)],
        scratch=[_wcopy(D_MODEL, COLB)] * 3,
        name="inproj_qkv", vmem_mb=52)(xn, w_in, w_in, w_in)


N_GATE_BLOCKS = 3 * D_MODEL // COLB


def _gate_body(x_ref, *refs):
    n = N_GATE_BLOCKS
    c_refs, o_ref, w_refs = refs[:n], refs[n], refs[n + 1:]

    @pl.when(_stage_weights(c_refs, w_refs, KW_WIDE))
    def _():
        x = x_ref[...]
        for j, w_ref in enumerate(w_refs):
            o_ref[:, j * COLB:(j + 1) * COLB] = jax.nn.sigmoid(_dot(x, w_ref[...])).astype(BF16)


def _gate_call(xn, w_in, l):
    n = N_GATE_BLOCKS
    return _call(
        _gate_body, (KW_WIDE + NT,),
        [_rows(D_MODEL, lag=KW_WIDE)] + [_wchunk(l, D_MODEL, COLB, KW_WIDE, 7 + j) for j in range(n)],
        _rows(3 * D_MODEL, lag=KW_WIDE), jax.ShapeDtypeStruct((ROWS, 3 * D_MODEL), BF16),
        scratch=[_wcopy(D_MODEL, COLB)] * n,
        name="inproj_gates", vmem_mb=56)(xn, *([w_in] * n))


def _merge_body(a_ref, b_ref, cp_ref, cs_ref, g0_ref, g1_ref, g2_ref, h_ref,
                ca_ref, cb_ref, cc_ref, co_ref, gf_ref, h1_ref, hn_ref,
                wa_ref, wb_ref, wc_ref, wo_ref):
    main = _stage_weights((ca_ref, cb_ref, cc_ref, co_ref), (wa_ref, wb_ref, wc_ref, wo_ref), KW_WIDE)

    @pl.when(main)
    def _():
        is_sample = pl.program_id(0) - KW_WIDE >= ROWS_P // TM_MERGE
        c = jnp.where(is_sample, cs_ref[...], cp_ref[...])
        m = g0_ref[...].astype(F32) * _dot(a_ref[...], wa_ref[...])
        m = m + g1_ref[...].astype(F32) * _dot(b_ref[...], wb_ref[...])
        m = m + g2_ref[...].astype(F32) * _dot(c, wc_ref[...])
        h1 = h_ref[...] + _dot(m.astype(BF16), wo_ref[...])
        h1_ref[...] = h1
        hn_ref[...] = _rms(h1, gf_ref[...]).astype(BF16)


def _merge_call(a, b, c_p, c_s, gates, h, w_a, w_b, w_c, w_o, g_ffn, l):
    tm, kw = TM_MERGE, KW_WIDE
    np_ = ROWS_P // tm
    rows = functools.partial(_rows, tm=tm, lag=kw)
    return _call(
        _merge_body, (kw + ROWS // tm,),
        [rows(CONV_DIM), rows(GMLP_DIM),
         pl.BlockSpec((tm, ATTN_DIM), lambda s: (jnp.clip(s - kw, 0, np_ - 1), 0)),
         pl.BlockSpec((tm, ATTN_DIM), lambda s: (jnp.maximum(s - kw - np_, 0), 0)),
         rows(D_MODEL, cb=0), rows(D_MODEL, cb=1), rows(D_MODEL, cb=2), rows(D_MODEL),
         _wchunk(l, CONV_DIM, D_MODEL, kw), _wchunk(l, GMLP_DIM, D_MODEL, kw),
         _wchunk(l, ATTN_DIM, D_MODEL, kw), _wchunk(l, D_MODEL, D_MODEL, kw), _vec(l, D_MODEL)],
        [rows(D_MODEL), rows(D_MODEL)],
        [jax.ShapeDtypeStruct((ROWS, D_MODEL), F32), jax.ShapeDtypeStruct((ROWS, D_MODEL), BF16)],
        scratch=[_wcopy(CONV_DIM, D_MODEL), _wcopy(GMLP_DIM, D_MODEL), _wcopy(ATTN_DIM, D_MODEL),
                 _wcopy(D_MODEL, D_MODEL)],
        name="merge_out", vmem_mb=58)(a, b, c_p, c_s, gates, gates, gates, h, w_a, w_b, w_c, w_o, g_ffn)


def _ffn_body(hn_ref, w1_ref, w2_ref, h1_ref, o_ref):
    @pl.when(pl.program_id(1) == 0)
    def _():
        o_ref[...] = h1_ref[...]

    a = jnp.square(jnp.maximum(_dot(hn_ref[...], w1_ref[...]), 0.0)).astype(BF16)
    o_ref[...] += _dot(a, w2_ref[...])


def _ffn_call(hn, h1, w1, w2, l):
    return _call(
        _ffn_body, (NT, D_FF // TF),
        [pl.BlockSpec((TM, D_MODEL), lambda i, f: (i, 0)),
         pl.BlockSpec((None, D_MODEL, TF), lambda i, f: (l, 0, f)),
         pl.BlockSpec((None, TF, D_MODEL), lambda i, f: (l, f, 0)),
         pl.BlockSpec((TM, D_MODEL), lambda i, f: (i, 0))],
        pl.BlockSpec((TM, D_MODEL), lambda i, f: (i, 0)),
        jax.ShapeDtypeStruct((ROWS, D_MODEL), F32),
        name="ffn", vmem_mb=56)(hn, w1, w2, h1)


def _ple_update(tile, h2_ref, pp_ref, ps_ref, wg_ref, wp_ref, gp_ref):
    h2 = h2_ref[...]
    p = jnp.where(tile == NT_P, ps_ref[...], pp_ref[...])
    gate = jax.nn.sigmoid(_dot(_rms(h2, gp_ref[...]).astype(BF16), wg_ref[...]))
    return h2 + gate * _dot(p.astype(BF16), wp_ref[...])


def _ple_body(h2_ref, pp_ref, ps_ref, cg_ref, cp_ref, gp_ref, gn_ref, h3_ref, xn_ref, wg_ref, wp_ref):
    @pl.when(_stage_weights((cg_ref, cp_ref), (wg_ref, wp_ref), KW))
    def _():
        h3 = _ple_update(pl.program_id(0) - KW, h2_ref, pp_ref, ps_ref, wg_ref, wp_ref, gp_ref)
        h3_ref[...] = h3
        xn_ref[...] = _rms(h3, gn_ref[...]).astype(BF16)


def _ple_last_body(h2_ref, pp_ref, ps_ref, cg_ref, cp_ref, gp_ref, gn_ref, yp_ref, ys_ref, wg_ref, wp_ref):
    main = _stage_weights((cg_ref, cp_ref), (wg_ref, wp_ref), KW)
    tile = pl.program_id(0) - KW

    def final():
        return _rms(_ple_update(tile, h2_ref, pp_ref, ps_ref, wg_ref, wp_ref, gp_ref), gn_ref[...])

    @pl.when(main & (tile < NT_P))
    def _():
        yp_ref[...] = final()

    @pl.when(tile >= NT_P)
    def _():
        ys_ref[...] = final()


def _ple_call(h2, p_p, p_s, w_g, w_p, g_ple, g_next, l, last):
    assert ROWS_S == TM
    ins = [_rows(D_MODEL, lag=KW),
           pl.BlockSpec((None, TM, PLE_DIM), lambda s: (l, jnp.clip(s - KW, 0, NT_P - 1), 0)),
           pl.BlockSpec((None, TM, PLE_DIM), lambda s: (l, 0, 0)),
           _wchunk(l, D_MODEL, D_MODEL, KW), _wchunk(l, PLE_DIM, D_MODEL, KW),
           _vec(l, D_MODEL), _vec(l + 1, D_MODEL)]
    scratch = [_wcopy(D_MODEL, D_MODEL), _wcopy(PLE_DIM, D_MODEL)]
    args = (h2, p_p, p_s, w_g, w_p, g_ple, g_next)
    if not last:
        return _call(
            _ple_body, (KW + NT,), ins, [_rows(D_MODEL, lag=KW)] * 2,
            [jax.ShapeDtypeStruct((ROWS, D_MODEL), F32), jax.ShapeDtypeStruct((ROWS, D_MODEL), BF16)],
            scratch=scratch, name="ple", vmem_mb=56)(*args)
    return _call(
        _ple_last_body, (KW + NT,), ins,
        [pl.BlockSpec((TM, D_MODEL), lambda s: (jnp.clip(s - KW, 0, NT_P - 1), 0)),
         pl.BlockSpec((TM, D_MODEL), lambda s: (0, 0))],
        [jax.ShapeDtypeStruct((ROWS_P, D_MODEL), F32), jax.ShapeDtypeStruct((ROWS_S, D_MODEL), F32)],
        scratch=scratch, name="ple_final", vmem_mb=56)(*args)


def _conv_rows(ext_ref, y_ref, cw_ref, ext_base, y_base, rc):
    for lb in range(CONV_DIM // CONV_LB):
        ls = slice(lb * CONV_LB, (lb + 1) * CONV_LB)
        acc = None
        for r in range(SUBLANES):
            rows = rc + (SUBLANES if r else 0)
            u = None
            for j in range(CONV_WIDTH):
                if (2 + j) % SUBLANES == r:
                    start = ext_base + (2 + j - r)
                    if not isinstance(start, int):
                        start = pl.multiple_of(start, SUBLANES)
                    term = cw_ref[j:j + 1, ls] * ext_ref[pl.ds(start, rows), ls]
                    u = term if u is None else u + term
            part = u[r:r + rc, :]
            acc = part if acc is None else acc + part
        y_ref[pl.ds(y_base, rc), ls] = acc


def _conv_body(a_ref, cache_ref, cw_ref, cb_ref, g_ref, b_ref, o_ref, ext_ref, y_ref):
    i = pl.program_id(0)
    hist = 32

    @pl.when(i < NT_P)
    def _():
        @pl.when(i % TILES_PER_SEQ == 0)
        def _():
            ext_ref[0:hist, :] = jnp.zeros((hist, CONV_DIM), F32)

        @pl.when(i % TILES_PER_SEQ != 0)
        def _():
            ext_ref[0:hist, :] = ext_ref[TM:TM + hist, :]

        ext_ref[hist:hist + TM, :] = a_ref[...]

        def chunk(c, carry):
            r0 = pl.multiple_of(c * CONV_RC, CONV_RC)
            _conv_rows(ext_ref, y_ref, cw_ref, r0, r0, CONV_RC)
            return carry

        lax.fori_loop(0, TM // CONV_RC, chunk, 0)

    @pl.when(i == NT_P)
    def _():
        def seq(s, carry):
            r0 = pl.multiple_of(s * DEC_SEQ, DEC_SEQ)
            ext_ref[hist - CONV_HIST:hist, :] = cache_ref[s]
            ext_ref[hist:hist + DEC_SEQ, :] = a_ref[pl.ds(r0, DEC_SEQ), :]
            _conv_rows(ext_ref, y_ref, cw_ref, 0, r0, DEC_SEQ)
            return carry

        lax.fori_loop(0, DEC_BATCH, seq, 0)

    y = _ln(y_ref[...] + cb_ref[...], g_ref[...], b_ref[...])
    o_ref[...] = (y * jax.nn.sigmoid(y)).astype(BF16)


def _conv_call(a, cache_conv, conv_w, conv_b, ln_g, ln_b, l):
    return _call(
        _conv_body, (NT,),
        [_rows(CONV_DIM),
         _resident((None, DEC_BATCH, CONV_HIST, CONV_DIM), (l, 0, 0, 0)),
         _resident((None, CONV_WIDTH, CONV_DIM), (l, 0, 0)),
         _vec(l, CONV_DIM), _vec(l, CONV_DIM), _vec(l, CONV_DIM)],
        _rows(CONV_DIM), jax.ShapeDtypeStruct((ROWS, CONV_DIM), BF16),
        scratch=[pltpu.VMEM((TM + 32, CONV_DIM), F32), pltpu.VMEM((TM, CONV_DIM), F32)],
        name="conv_mixer", vmem_mb=32)(a, cache_conv, conv_w, conv_b, ln_g, ln_b)


def _toeplitz_rows(tab_ref, h, rows):
    t = jnp.broadcast_to(tab_ref[h:h + 1, :], (rows, TOEP))
    return pltpu.roll(t, TOEP - QB + 1, 1, stride=1, stride_axis=0)


def _attn_p_body(q_ref, k0_ref, k1_ref, k2_ref, v0_ref, v1_ref, v2_ref, tab_ref, o_ref, bias_ref):
    m = pl.program_id(1)
    k_refs = (k0_ref, k1_ref, k2_ref)
    v_refs = (v0_ref, v1_ref, v2_ref)

    @pl.when((pl.program_id(0) == 0) & (m == 0))
    def _():
        r = lax.broadcasted_iota(jnp.int32, (QB, KB), 0) // CHUNK
        j = lax.broadcasted_iota(jnp.int32, (QB, KB), 1) // CHUNK
        band = (j >= r) & (j <= r + LEFT_CHUNKS)
        for h in range(N_HEADS):
            bias = jnp.where(band, _toeplitz_rows(tab_ref, h, QB)[:, :KB], NEG)
            for s in range(3):
                bias_ref[h, s] = bias[:, s * QB:(s + 1) * QB]
            bias_ref[h, 3] = jnp.full((QB, QB), NEG, F32)

    slot = [jnp.where(m - 2 + s >= 0, s, 3) for s in range(3)]
    for h in range(N_HEADS):
        hs = slice(h * HEAD_DIM, (h + 1) * HEAD_DIM)
        qh = q_ref[:, hs]
        sc = [_dot_nt(qh, k_refs[s][:, hs]) + bias_ref[h, slot[s]] for s in range(3)]
        mx = jnp.max(functools.reduce(jnp.maximum, sc), axis=-1, keepdims=True)
        e = [jnp.exp(x - mx) for x in sc]
        den = jnp.sum(functools.reduce(jnp.add, e), axis=-1, keepdims=True)
        o = functools.reduce(jnp.add, [_dot(e[s].astype(BF16), v_refs[s][:, hs]) for s in range(3)])
        o_ref[:, hs] = (o * (1.0 / den)).astype(BF16)


def _attn_p_call(q, k, v, toep, l):
    nq = SEQ // QB

    def kv_spec(s):
        return pl.BlockSpec((QB, ATTN_DIM), lambda b, m: (b * nq + jnp.maximum(m - 2 + s, 0), 0))

    return _call(
        _attn_p_body, (BATCH, nq),
        [pl.BlockSpec((QB, ATTN_DIM), lambda b, m: (b * nq + m, 0))]
        + [kv_spec(s) for s in range(3)] + [kv_spec(s) for s in range(3)]
        + [_resident((None, N_HEADS, TOEP), (l, 0, 0))],
        pl.BlockSpec((QB, ATTN_DIM), lambda b, m: (b * nq + m, 0)),
        jax.ShapeDtypeStruct((ROWS_P, ATTN_DIM), BF16),
        scratch=[pltpu.VMEM((N_HEADS, 4, QB, QB), F32)],
        name="attn_prompt", vmem_mb=40)(q, k, k, k, v, v, v, toep)


def _attn_s_body(q_ref, kn_ref, vn_ref, kc_ref, vc_ref, tab_ref, o_ref, bias_ref):
    w = ATTN_REACH

    @pl.when(pl.program_id(0) == 0)
    def _():
        for h in range(N_HEADS):
            bias_ref[h] = _toeplitz_rows(tab_ref, h, DEC_SEQ)[:, :w + DEC_SEQ]

    for b in range(SEQS_PER_STEP):
        rs = slice(b * DEC_SEQ, (b + 1) * DEC_SEQ)
        for h in range(N_HEADS):
            hs = slice(h * HEAD_DIM, (h + 1) * HEAD_DIM)
            rows_h = pl.ds(h, w, stride=N_HEADS)
            qh = q_ref[rs, hs]
            s_c = _dot_nt(qh, kc_ref[b, rows_h, :].astype(BF16)) + bias_ref[h, :, :w]
            s_n = _dot_nt(qh, kn_ref[rs, hs]) + bias_ref[h, :, w:]
            mx = jnp.maximum(jnp.max(s_c, axis=-1, keepdims=True), jnp.max(s_n, axis=-1, keepdims=True))
            e_c = jnp.exp(s_c - mx)
            e_n = jnp.exp(s_n - mx)
            den = jnp.sum(e_c, axis=-1, keepdims=True) + jnp.sum(e_n, axis=-1, keepdims=True)
            o = (_dot(e_c.astype(BF16), vc_ref[b, rows_h, :].astype(BF16))
                 + _dot(e_n.astype(BF16), vn_ref[rs, hs]))
            o_ref[rs, hs] = (o * (1.0 / den)).astype(BF16)


def _attn_s_call(q, k, v, cache_k, cache_v, toep, l):
    nb, rows = SEQS_PER_STEP, SEQS_PER_STEP * DEC_SEQ
    first = ROWS_P // rows
    new = pl.BlockSpec((rows, ATTN_DIM), lambda i: (first + i, 0))
    cache = pl.BlockSpec((None, nb, ATTN_REACH * N_HEADS, HEAD_DIM), lambda i: (l, i, 0, 0))
    return _call(
        _attn_s_body, (DEC_BATCH // nb,),
        [new, new, new, cache, cache, _resident((None, N_HEADS, TOEP), (l, 0, 0))],
        pl.BlockSpec((rows, ATTN_DIM), lambda i: (i, 0)),
        jax.ShapeDtypeStruct((ROWS_S, ATTN_DIM), BF16),
        scratch=[pltpu.VMEM((N_HEADS, DEC_SEQ, ATTN_REACH + DEC_SEQ), F32)],
        name="attn_sample", vmem_mb=40)(q, k, v, cache_k, cache_v, toep)


def _toeplitz_table(rel_tab):
    left = 3 * QB - 1 - REL_CLIP
    right = TOEP - left - (2 * REL_CLIP + 1)
    pad = [(0, 0)] * (rel_tab.ndim - 1) + [(left, right)]
    return jnp.pad(rel_tab[..., ::-1], pad, mode="edge").astype(F32)


def kernel(x_prompt, x_sample, p_prompt, p_sample, cache_conv, cache_k, cache_v, norm_mix, w_in, conv_w, conv_b, conv_ln_g, conv_ln_b, w_a_out, gmlp_ln_g, gmlp_ln_b, gmlp_ws, gmlp_bs, w_b_out, attn_rel_bias, w_c_out, w_o, norm_ffn, w_ff1, w_ff2, norm_ple, w_ple_gate, w_ple_proj, norm_final):
    p_p = p_prompt.reshape(DEPTH, ROWS_P, PLE_DIM)
    p_s = p_sample.reshape(DEPTH, ROWS_S, PLE_DIM)
    cache_k = cache_k.reshape(DEPTH, DEC_BATCH, ATTN_REACH * N_HEADS, HEAD_DIM)
    cache_v = cache_v.reshape(DEPTH, DEC_BATCH, ATTN_REACH * N_HEADS, HEAD_DIM)
    toep = _toeplitz_table(attn_rel_bias)

    vec = lambda x: x.reshape(x.shape[0], 1, x.shape[-1])
    g_mix = vec(jnp.concatenate([norm_mix, norm_final[None]], axis=0))
    conv_b3, conv_g3, conv_lb3 = vec(conv_b), vec(conv_ln_g), vec(conv_ln_b)
    gm_g3, gm_b3 = vec(gmlp_ln_g), vec(gmlp_ln_b)
    g_ffn3, g_ple3 = vec(norm_ffn), vec(norm_ple)

    w_ff1_b, w_ff2_b = w_ff1.astype(BF16), w_ff2.astype(BF16)

    outs = [[] for _ in range(7)]
    h, xn = _first_call(x_prompt.reshape(ROWS_P, D_MODEL), x_sample.reshape(ROWS_S, D_MODEL), g_mix)
    for l in range(DEPTH):
        a = _glu_call(xn, w_in, l)
        q, k, v, kv32 = _qkv_call(xn, w_in, l)
        gates = _gate_call(xn, w_in, l)

        ya_in = _conv_call(a, cache_conv, conv_w, conv_b3, conv_g3, conv_lb3, l)

        ws = gmlp_ws[l]
        corner = jnp.tile(ws[:, :DEC_SEQ, :DEC_SEQ], (1, GMLP_CHUNK // DEC_SEQ, GMLP_CHUNK // DEC_SEQ))
        bias_p = jnp.repeat(gmlp_bs[l].T, GMLP_GC, axis=1)
        bias_s = jnp.tile(bias_p[:DEC_SEQ], (GMLP_CHUNK // DEC_SEQ, 1))
        yb_in, v_s = _gmlp_call(xn, w_in, gm_g3, gm_b3, jnp.stack([ws, corner]), jnp.stack([bias_p, bias_s]), l)

        o_p = _attn_p_call(q, k, v, toep, l)
        o_s = _attn_s_call(q, k, v, cache_k, cache_v, toep, l)

        h1, hn = _merge_call(ya_in, yb_in, o_p, o_s, gates, h, w_a_out, w_b_out, w_c_out, w_o, g_ffn3, l)
        h2 = _ffn_call(hn, h1, w_ff1_b, w_ff2_b, l)
        last = l == DEPTH - 1
        h, xn = _ple_call(h2, p_p, p_s, w_ple_gate, w_ple_proj, g_ple3, g_mix, l, last)

        outs[0].append(jnp.stack([a[(b + 1) * SEQ - CONV_HIST:(b + 1) * SEQ] for b in range(BATCH)]))
        outs[1].append(a[ROWS_P:].reshape(DEC_BATCH, DEC_SEQ, CONV_DIM)[:, DEC_SEQ - CONV_HIST:])
        kv_p = kv32[:BATCH * TM].reshape(BATCH, TM, 2, N_HEADS, HEAD_DIM)
        kv_s = kv32[BATCH * TM:].reshape(DEC_BATCH, DEC_SEQ, 2, N_HEADS, HEAD_DIM)
        outs[2].append(kv_p[:, :, 0])
        outs[3].append(kv_p[:, :, 1])
        outs[4].append(kv_s[:, :, 0])
        outs[5].append(kv_s[:, :, 1])
        outs[6].append(v_s.reshape(DEC_BATCH, DEC_SEQ, GMLP_DIM))

    y_prompt = h.reshape(BATCH, SEQ, D_MODEL)
    y_sample = xn.reshape(DEC_BATCH, DEC_SEQ, D_MODEL)
    return (y_prompt, y_sample) + tuple(jnp.stack(x) for x in outs)
```

```python
import functools

import jax
import jax.numpy as jnp
from jax import lax
from jax.experimental import pallas as pl
from jax.experimental.pallas import tpu as pltpu

F32 = jnp.float32
BF16 = jnp.bfloat16

D_MODEL = 2048
BATCH = 2
SEQ = 4096
DEPTH = 4
DEC_BATCH = 16
DEC_SEQ = 32
CHUNK = 64
CONV_DIM = 1024
CONV_WIDTH = 31
CONV_HIST = CONV_WIDTH - 1
GMLP_DIM = 1024
GMLP_GROUPS = 8
GMLP_GC = GMLP_DIM // GMLP_GROUPS
GMLP_CHUNK = 128
N_HEADS = 8
HEAD_DIM = 128
ATTN_DIM = N_HEADS * HEAD_DIM
LEFT_CHUNKS = 8
ATTN_REACH = LEFT_CHUNKS * CHUNK
REL_CLIP = 128
D_FF = 4 * D_MODEL
PLE_DIM = 256
EPS = 1e-6
SCALE = HEAD_DIM ** -0.5
LOG2E = 1.4426950408889634
NEG = -1e30

ROWS_P = BATCH * SEQ
ROWS_S = DEC_BATCH * DEC_SEQ
ROWS = ROWS_P + ROWS_S
TM = 512
NT = ROWS // TM
NT_P = ROWS_P // TM
TILES_PER_SEQ = SEQ // TM
TM_MERGE = 256
TF = 1024
COLB = 1024
QB = 256
KB = 3 * QB
CONV_RC = 64
CONV_LB = 128
SUBLANES = 8
TOEP = 1024
KW = 8
KW_WIDE = 16
SEQS_PER_STEP = 2
MIB = 1 << 20


def _rms(x, g):
    return x * lax.rsqrt(jnp.mean(x * x, axis=-1, keepdims=True) + EPS) * g


def _ln(x, g, b):
    mu = jnp.mean(x, axis=-1, keepdims=True)
    xc = x - mu
    var = jnp.mean(xc * xc, axis=-1, keepdims=True)
    return xc * lax.rsqrt(var + EPS) * g + b


def _dot(a, b):
    return jnp.dot(a, b, preferred_element_type=F32)


def _dot_nt(a, b):
    return lax.dot_general(a, b, (((1,), (1,)), ((), ())), preferred_element_type=F32)


def _call(body, grid, in_specs, out_specs, out_shape, *, name, vmem_mb, scratch=()):
    return pl.pallas_call(
        body,
        grid=grid,
        in_specs=in_specs,
        out_specs=out_specs,
        out_shape=out_shape,
        scratch_shapes=list(scratch),
        compiler_params=pltpu.CompilerParams(
            dimension_semantics=("arbitrary",) * len(grid), vmem_limit_bytes=vmem_mb * MIB),
        name=name,
    )


def _rows(cols, tm=TM, cb=0, lag=0):
    return pl.BlockSpec((tm, cols), lambda s, *_: (jnp.maximum(s - lag, 0), cb))


def _resident(block, index):
    return pl.BlockSpec(block, lambda *_: index, pipeline_mode=pl.Buffered(1))


def _vec(l, n):
    return _resident((None, 1, n), (l, 0, 0))


def _wchunk(l, k, cols, kw, cb=0):
    return pl.BlockSpec((None, k // kw, cols), lambda s, *_: (l, jnp.minimum(s, kw - 1), cb))


def _wcopy(k, cols):
    return pltpu.VMEM((k, cols), BF16)


def _stage_weights(chunk_refs, w_refs, kw):
    s = pl.program_id(0)

    @pl.when(s < kw)
    def _():
        for c_ref, w_ref in zip(chunk_refs, w_refs):
            rows = c_ref.shape[0]
            w_ref[pl.ds(pl.multiple_of(s * rows, rows), rows), :] = c_ref[...].astype(BF16)

    return s >= kw


def _group_specs(cols):
    assert ROWS_S == TM
    return [pl.BlockSpec((TM, cols), lambda i: (jnp.minimum(i, NT_P - 1), 0)),
            pl.BlockSpec((TM, cols), lambda i: (0, 0))]


def _first_body(xp_ref, xs_ref, g_ref, h_ref, xn_ref):
    x = jnp.where(pl.program_id(0) == NT_P, xs_ref[...], xp_ref[...])
    h_ref[...] = x
    xn_ref[...] = _rms(x, g_ref[...]).astype(BF16)


def _first_call(xp, xs, g3):
    return _call(
        _first_body, (NT,), _group_specs(D_MODEL) + [_vec(0, D_MODEL)], [_rows(D_MODEL)] * 2,
        [jax.ShapeDtypeStruct((ROWS, D_MODEL), F32), jax.ShapeDtypeStruct((ROWS, D_MODEL), BF16)],
        name="rms_first", vmem_mb=40)(xp, xs, g3)


def _glu_gmlp_body(x_ref, ca_ref, cg_ref, cu_ref, cv_ref, g_ref, b_ref, ws_ref, bias_ref,
                   a_ref, yb_ref, vs_ref, wa_ref, wg_ref, wu_ref, wv_ref):
    main = _stage_weights((ca_ref, cg_ref, cu_ref, cv_ref), (wa_ref, wg_ref, wu_ref, wv_ref), KW_WIDE)

    @pl.when(main)
    def _():
        x = x_ref[...]
        v = _ln(jax.nn.gelu(_dot(x, wv_ref[...])), g_ref[...], b_ref[...])
        vs_ref[...] = v
        u = jax.nn.gelu(_dot(x, wu_ref[...]))
        a_ref[...] = _dot(x, wa_ref[...]) * jax.nn.sigmoid(_dot(x, wg_ref[...]))
        shift = jnp.where(pl.program_id(0) - KW_WIDE == NT_P, 5, 7)
        r = lax.broadcasted_iota(jnp.int32, (GMLP_CHUNK, GMLP_CHUNK), 0)
        c = lax.broadcasted_iota(jnp.int32, (GMLP_CHUNK, GMLP_CHUNK), 1)
        mask = (r >= c) & (lax.shift_right_logical(r, shift) == lax.shift_right_logical(c, shift))
        for g in range(GMLP_GROUPS):
            cs = slice(g * GMLP_GC, (g + 1) * GMLP_GC)
            wg = jnp.where(mask, ws_ref[g], 0.0).astype(BF16)
            bias = bias_ref[:, cs]
            for k in range(TM // GMLP_CHUNK):
                rs = slice(k * GMLP_CHUNK, (k + 1) * GMLP_CHUNK)
                mixed = _dot(wg, v[rs, cs].astype(BF16)) + bias
                yb_ref[rs, cs] = (u[rs, cs] * mixed).astype(BF16)


def _glu_gmlp_call(xn, w_in, ln_g, ln_b, wmix, bmix, l):
    assert ROWS_S == TM
    kw = KW_WIDE
    pick = lambda s: (jnp.where(s - kw == NT_P, 1, 0), 0, 0, 0)
    return _call(
        _glu_gmlp_body, (kw + NT,),
        [_rows(D_MODEL, lag=kw)] + [_wchunk(l, D_MODEL, COLB, kw, j) for j in range(4)]
        + [_vec(l, GMLP_DIM), _vec(l, GMLP_DIM),
           pl.BlockSpec((None, GMLP_GROUPS, GMLP_CHUNK, GMLP_CHUNK), pick),
           pl.BlockSpec((None, GMLP_CHUNK, GMLP_DIM), lambda s: pick(s)[:3])],
        [_rows(CONV_DIM, lag=kw), _rows(GMLP_DIM, lag=kw), pl.BlockSpec((TM, GMLP_DIM), lambda s: (0, 0))],
        [jax.ShapeDtypeStruct((ROWS, CONV_DIM), F32), jax.ShapeDtypeStruct((ROWS, GMLP_DIM), BF16),
         jax.ShapeDtypeStruct((ROWS_S, GMLP_DIM), F32)],
        scratch=[_wcopy(D_MODEL, COLB)] * 4,
        name="inproj_glu_gmlp", vmem_mb=58)(xn, w_in, w_in, w_in, w_in, ln_g, ln_b, wmix, bmix)


KV_SLOTS = BATCH + 1


def _qkv_body(x_ref, cq_ref, ck_ref, cv_ref, q_ref, k_ref, v_ref, kr_ref, vr_ref, wq_ref, wk_ref, wv_ref):
    @pl.when(_stage_weights((cq_ref, ck_ref, cv_ref), (wq_ref, wk_ref, wv_ref), KW))
    def _():
        x = x_ref[...]
        q_ref[...] = (_dot(x, wq_ref[...]) * (SCALE * LOG2E)).astype(BF16)
        for w_ref, o_ref, rows_ref in ((wk_ref, k_ref, kr_ref), (wv_ref, v_ref, vr_ref)):
            y = _dot(x, w_ref[...])
            o_ref[...] = y.astype(BF16)
            for h in range(N_HEADS):
                rows_ref[pl.ds(h, TM, stride=N_HEADS), :] = y[:, h * HEAD_DIM:(h + 1) * HEAD_DIM]


def _qkv_call(xn, w_in, l):
    act = jax.ShapeDtypeStruct((ROWS, ATTN_DIM), BF16)
    rows = jax.ShapeDtypeStruct((KV_SLOTS * TM * N_HEADS, HEAD_DIM), F32)
    kv_slot = lambda s: (jnp.maximum(s - KW, 0) // TILES_PER_SEQ, 0)
    return _call(
        _qkv_body, (KW + NT,),
        [_rows(D_MODEL, lag=KW)] + [_wchunk(l, D_MODEL, COLB, KW, 4 + j) for j in range(3)],
        [_rows(ATTN_DIM, lag=KW)] * 3 + [pl.BlockSpec((TM * N_HEADS, HEAD_DIM), kv_slot)] * 2,
        [act, act, act, rows, rows],
        scratch=[_wcopy(D_MODEL, COLB)] * 3,
        name="inproj_qkv", vmem_mb=52)(xn, w_in, w_in, w_in)


N_GATE_BLOCKS = 3 * D_MODEL // COLB


def _gate_body(x_ref, *refs):
    n = N_GATE_BLOCKS
    c_refs, o_ref, w_refs = refs[:n], refs[n], refs[n + 1:]

    @pl.when(_stage_weights(c_refs, w_refs, KW_WIDE))
    def _():
        x = x_ref[...]
        for j, w_ref in enumerate(w_refs):
            o_ref[:, j * COLB:(j + 1) * COLB] = jax.nn.sigmoid(_dot(x, w_ref[...])).astype(BF16)


def _gate_call(xn, w_in, l):
    n = N_GATE_BLOCKS
    return _call(
        _gate_body, (KW_WIDE + NT,),
        [_rows(D_MODEL, lag=KW_WIDE)] + [_wchunk(l, D_MODEL, COLB, KW_WIDE, 7 + j) for j in range(n)],
        _rows(3 * D_MODEL, lag=KW_WIDE), jax.ShapeDtypeStruct((ROWS, 3 * D_MODEL), BF16),
        scratch=[_wcopy(D_MODEL, COLB)] * n,
        name="inproj_gates", vmem_mb=56)(xn, *([w_in] * n))


def _merge_body(a_ref, b_ref, cp_ref, cs_ref, g0_ref, g1_ref, g2_ref, h_ref,
                ca_ref, cb_ref, cc_ref, co_ref, gf_ref, h1_ref, hn_ref,
                wa_ref, wb_ref, wc_ref, wo_ref):
    main = _stage_weights((ca_ref, cb_ref, cc_ref, co_ref), (wa_ref, wb_ref, wc_ref, wo_ref), KW_WIDE)

    @pl.when(main)
    def _():
        is_sample = pl.program_id(0) - KW_WIDE >= ROWS_P // TM_MERGE
        c = jnp.where(is_sample, cs_ref[...], cp_ref[...])
        m = g0_ref[...].astype(F32) * _dot(a_ref[...], wa_ref[...])
        m = m + g1_ref[...].astype(F32) * _dot(b_ref[...], wb_ref[...])
        m = m + g2_ref[...].astype(F32) * _dot(c, wc_ref[...])
        h1 = h_ref[...] + _dot(m.astype(BF16), wo_ref[...])
        h1_ref[...] = h1
        hn_ref[...] = _rms(h1, gf_ref[...]).astype(BF16)


def _merge_call(a, b, c_p, c_s, gates, h, w_a, w_b, w_c, w_o, g_ffn, l):
    tm, kw = TM_MERGE, KW_WIDE
    np_ = ROWS_P // tm
    rows = functools.partial(_rows, tm=tm, lag=kw)
    return _call(
        _merge_body, (kw + ROWS // tm,),
        [rows(CONV_DIM), rows(GMLP_DIM),
         pl.BlockSpec((tm, ATTN_DIM), lambda s: (jnp.clip(s - kw, 0, np_ - 1), 0)),
         pl.BlockSpec((tm, ATTN_DIM), lambda s: (jnp.maximum(s - kw - np_, 0), 0)),
         rows(D_MODEL, cb=0), rows(D_MODEL, cb=1), rows(D_MODEL, cb=2), rows(D_MODEL),
         _wchunk(l, CONV_DIM, D_MODEL, kw), _wchunk(l, GMLP_DIM, D_MODEL, kw),
         _wchunk(l, ATTN_DIM, D_MODEL, kw), _wchunk(l, D_MODEL, D_MODEL, kw), _vec(l, D_MODEL)],
        [rows(D_MODEL), rows(D_MODEL)],
        [jax.ShapeDtypeStruct((ROWS, D_MODEL), F32), jax.ShapeDtypeStruct((ROWS, D_MODEL), BF16)],
        scratch=[_wcopy(CONV_DIM, D_MODEL), _wcopy(GMLP_DIM, D_MODEL), _wcopy(ATTN_DIM, D_MODEL),
                 _wcopy(D_MODEL, D_MODEL)],
        name="merge_out", vmem_mb=58)(a, b, c_p, c_s, gates, gates, gates, h, w_a, w_b, w_c, w_o, g_ffn)


def _ffn_body(hn_ref, w1_ref, w2_ref, h1_ref, o_ref):
    def delta():
        a = jnp.square(jnp.maximum(_dot(hn_ref[...], w1_ref[...]), 0.0)).astype(BF16)
        return _dot(a, w2_ref[...])

    @pl.when(pl.program_id(1) == 0)
    def _():
        o_ref[...] = h1_ref[...] + delta()

    @pl.when(pl.program_id(1) != 0)
    def _():
        o_ref[...] += delta()


def _ffn_call(hn, h1, w1, w2, l):
    return _call(
        _ffn_body, (NT, D_FF // TF),
        [pl.BlockSpec((TM, D_MODEL), lambda i, f: (i, 0)),
         pl.BlockSpec((None, D_MODEL, TF), lambda i, f: (l, 0, f)),
         pl.BlockSpec((None, TF, D_MODEL), lambda i, f: (l, f, 0)),
         pl.BlockSpec((TM, D_MODEL), lambda i, f: (i, 0))],
        pl.BlockSpec((TM, D_MODEL), lambda i, f: (i, 0)),
        jax.ShapeDtypeStruct((ROWS, D_MODEL), F32),
        name="ffn", vmem_mb=56)(hn, w1, w2, h1)


def _ple_update(tile, h2_ref, pp_ref, ps_ref, wg_ref, wp_ref, gp_ref):
    h2 = h2_ref[...]
    p = jnp.where(tile == NT_P, ps_ref[...], pp_ref[...])
    gate = jax.nn.sigmoid(_dot(_rms(h2, gp_ref[...]).astype(BF16), wg_ref[...]))
    return h2 + gate * _dot(p.astype(BF16), wp_ref[...])


def _ple_body(h2_ref, pp_ref, ps_ref, cg_ref, cp_ref, gp_ref, gn_ref, h3_ref, xn_ref, wg_ref, wp_ref):
    @pl.when(_stage_weights((cg_ref, cp_ref), (wg_ref, wp_ref), KW))
    def _():
        h3 = _ple_update(pl.program_id(0) - KW, h2_ref, pp_ref, ps_ref, wg_ref, wp_ref, gp_ref)
        h3_ref[...] = h3
        xn_ref[...] = _rms(h3, gn_ref[...]).astype(BF16)


def _ple_last_body(h2_ref, pp_ref, ps_ref, cg_ref, cp_ref, gp_ref, gn_ref, yp_ref, ys_ref, wg_ref, wp_ref):
    main = _stage_weights((cg_ref, cp_ref), (wg_ref, wp_ref), KW)
    tile = pl.program_id(0) - KW

    def final():
        return _rms(_ple_update(tile, h2_ref, pp_ref, ps_ref, wg_ref, wp_ref, gp_ref), gn_ref[...])

    @pl.when(main & (tile < NT_P))
    def _():
        yp_ref[...] = final()

    @pl.when(tile >= NT_P)
    def _():
        ys_ref[...] = final()


def _ple_call(h2, p_p, p_s, w_g, w_p, g_ple, g_next, l, last):
    assert ROWS_S == TM
    ins = [_rows(D_MODEL, lag=KW),
           pl.BlockSpec((None, TM, PLE_DIM), lambda s: (l, jnp.clip(s - KW, 0, NT_P - 1), 0)),
           pl.BlockSpec((None, TM, PLE_DIM), lambda s: (l, 0, 0)),
           _wchunk(l, D_MODEL, D_MODEL, KW), _wchunk(l, PLE_DIM, D_MODEL, KW),
           _vec(l, D_MODEL), _vec(l + 1, D_MODEL)]
    scratch = [_wcopy(D_MODEL, D_MODEL), _wcopy(PLE_DIM, D_MODEL)]
    args = (h2, p_p, p_s, w_g, w_p, g_ple, g_next)
    if not last:
        return _call(
            _ple_body, (KW + NT,), ins, [_rows(D_MODEL, lag=KW)] * 2,
            [jax.ShapeDtypeStruct((ROWS, D_MODEL), F32), jax.ShapeDtypeStruct((ROWS, D_MODEL), BF16)],
            scratch=scratch, name="ple", vmem_mb=56)(*args)
    return _call(
        _ple_last_body, (KW + NT,), ins,
        [pl.BlockSpec((TM, D_MODEL), lambda s: (jnp.clip(s - KW, 0, NT_P - 1), 0)),
         pl.BlockSpec((TM, D_MODEL), lambda s: (0, 0))],
        [jax.ShapeDtypeStruct((ROWS_P, D_MODEL), F32), jax.ShapeDtypeStruct((ROWS_S, D_MODEL), F32)],
        scratch=scratch, name="ple_final", vmem_mb=56)(*args)


def _conv_rows(ext_ref, y_ref, cw_ref, ext_base, y_base, rc):
    for lb in range(CONV_DIM // CONV_LB):
        ls = slice(lb * CONV_LB, (lb + 1) * CONV_LB)
        acc = None
        for r in range(SUBLANES):
            rows = rc + (SUBLANES if r else 0)
            u = None
            for j in range(CONV_WIDTH):
                if (2 + j) % SUBLANES == r:
                    start = ext_base + (2 + j - r)
                    if not isinstance(start, int):
                        start = pl.multiple_of(start, SUBLANES)
                    term = cw_ref[j:j + 1, ls] * ext_ref[pl.ds(start, rows), ls]
                    u = term if u is None else u + term
            part = u[r:r + rc, :]
            acc = part if acc is None else acc + part
        y_ref[pl.ds(y_base, rc), ls] = acc


def _conv_body(a_ref, cache_ref, cw_ref, cb_ref, g_ref, b_ref, o_ref, ext_ref, y_ref):
    i = pl.program_id(0)
    hist = 32

    @pl.when(i < NT_P)
    def _():
        @pl.when(i % TILES_PER_SEQ == 0)
        def _():
            ext_ref[0:hist, :] = jnp.zeros((hist, CONV_DIM), F32)

        @pl.when(i % TILES_PER_SEQ != 0)
        def _():
            ext_ref[0:hist, :] = ext_ref[TM:TM + hist, :]

        ext_ref[hist:hist + TM, :] = a_ref[...]

        def chunk(c, carry):
            r0 = pl.multiple_of(c * CONV_RC, CONV_RC)
            _conv_rows(ext_ref, y_ref, cw_ref, r0, r0, CONV_RC)
            return carry

        lax.fori_loop(0, TM // CONV_RC, chunk, 0)

    @pl.when(i == NT_P)
    def _():
        def seq(s, carry):
            r0 = pl.multiple_of(s * DEC_SEQ, DEC_SEQ)
            ext_ref[hist - CONV_HIST:hist, :] = cache_ref[s]
            ext_ref[hist:hist + DEC_SEQ, :] = a_ref[pl.ds(r0, DEC_SEQ), :]
            _conv_rows(ext_ref, y_ref, cw_ref, 0, r0, DEC_SEQ)
            return carry

        lax.fori_loop(0, DEC_BATCH, seq, 0)

    y = _ln(y_ref[...] + cb_ref[...], g_ref[...], b_ref[...])
    o_ref[...] = (y * jax.nn.sigmoid(y)).astype(BF16)


def _conv_call(a, cache_conv, conv_w, conv_b, ln_g, ln_b, l):
    return _call(
        _conv_body, (NT,),
        [_rows(CONV_DIM),
         _resident((None, DEC_BATCH, CONV_HIST, CONV_DIM), (l, 0, 0, 0)),
         _resident((None, CONV_WIDTH, CONV_DIM), (l, 0, 0)),
         _vec(l, CONV_DIM), _vec(l, CONV_DIM), _vec(l, CONV_DIM)],
        _rows(CONV_DIM), jax.ShapeDtypeStruct((ROWS, CONV_DIM), BF16),
        scratch=[pltpu.VMEM((TM + 32, CONV_DIM), F32), pltpu.VMEM((TM, CONV_DIM), F32)],
        name="conv_mixer", vmem_mb=32)(a, cache_conv, conv_w, conv_b, ln_g, ln_b)


def _toeplitz_rows(tab_ref, h, rows):
    t = jnp.broadcast_to(tab_ref[h:h + 1, :], (rows, TOEP))
    return pltpu.roll(t, TOEP - QB + 1, 1, stride=1, stride_axis=0)


def _attn_p_body(q_ref, k0_ref, k1_ref, k2_ref, v0_ref, v1_ref, v2_ref, tab_ref, o_ref, bias_ref):
    m = pl.program_id(1)
    k_refs = (k0_ref, k1_ref, k2_ref)
    v_refs = (v0_ref, v1_ref, v2_ref)

    @pl.when((pl.program_id(0) == 0) & (m == 0))
    def _():
        r = lax.broadcasted_iota(jnp.int32, (QB, KB), 0) // CHUNK
        j = lax.broadcasted_iota(jnp.int32, (QB, KB), 1) // CHUNK
        band = (j >= r) & (j <= r + LEFT_CHUNKS)
        for h in range(N_HEADS):
            bias = jnp.where(band, _toeplitz_rows(tab_ref, h, QB)[:, :KB], NEG)
            for s in range(3):
                bias_ref[h, s] = bias[:, s * QB:(s + 1) * QB]
            bias_ref[h, 3] = jnp.full((QB, QB), NEG, F32)

    slot = [jnp.where(m - 2 + s >= 0, s, 3) for s in range(3)]
    for h in range(N_HEADS):
        hs = slice(h * HEAD_DIM, (h + 1) * HEAD_DIM)
        qh = q_ref[:, hs]
        sc = [_dot_nt(qh, k_refs[s][:, hs]) + bias_ref[h, slot[s]] for s in range(3)]
        mx = jnp.max(functools.reduce(jnp.maximum, sc), axis=-1, keepdims=True)
        e = [jnp.exp2(x - mx) for x in sc]
        den = jnp.sum(functools.reduce(jnp.add, e), axis=-1, keepdims=True)
        o = functools.reduce(jnp.add, [_dot(e[s].astype(BF16), v_refs[s][:, hs]) for s in range(3)])
        o_ref[:, hs] = (o * (1.0 / den)).astype(BF16)


def _attn_p_call(q, k, v, toep, l):
    nq = SEQ // QB

    def kv_spec(s):
        return pl.BlockSpec((QB, ATTN_DIM), lambda b, m: (b * nq + jnp.maximum(m - 2 + s, 0), 0))

    return _call(
        _attn_p_body, (BATCH, nq),
        [pl.BlockSpec((QB, ATTN_DIM), lambda b, m: (b * nq + m, 0))]
        + [kv_spec(s) for s in range(3)] + [kv_spec(s) for s in range(3)]
        + [_resident((None, N_HEADS, TOEP), (l, 0, 0))],
        pl.BlockSpec((QB, ATTN_DIM), lambda b, m: (b * nq + m, 0)),
        jax.ShapeDtypeStruct((ROWS_P, ATTN_DIM), BF16),
        scratch=[pltpu.VMEM((N_HEADS, 4, QB, QB), F32)],
        name="attn_prompt", vmem_mb=40)(q, k, k, k, v, v, v, toep)


def _attn_s_body(q_ref, kn_ref, vn_ref, kc_ref, vc_ref, tab_ref, o_ref, bias_ref):
    w = ATTN_REACH

    @pl.when(pl.program_id(0) == 0)
    def _():
        for h in range(N_HEADS):
            bias_ref[h] = _toeplitz_rows(tab_ref, h, DEC_SEQ)[:, :w + DEC_SEQ]

    for b in range(SEQS_PER_STEP):
        rs = slice(b * DEC_SEQ, (b + 1) * DEC_SEQ)
        for h in range(N_HEADS):
            hs = slice(h * HEAD_DIM, (h + 1) * HEAD_DIM)
            rows_h = pl.ds(h, w, stride=N_HEADS)
            qh = q_ref[rs, hs]
            s_c = _dot_nt(qh, kc_ref[b, rows_h, :].astype(BF16)) + bias_ref[h, :, :w]
            s_n = _dot_nt(qh, kn_ref[rs, hs]) + bias_ref[h, :, w:]
            mx = jnp.maximum(jnp.max(s_c, axis=-1, keepdims=True), jnp.max(s_n, axis=-1, keepdims=True))
            e_c = jnp.exp2(s_c - mx)
            e_n = jnp.exp2(s_n - mx)
            den = jnp.sum(e_c, axis=-1, keepdims=True) + jnp.sum(e_n, axis=-1, keepdims=True)
            o = (_dot(e_c.astype(BF16), vc_ref[b, rows_h, :].astype(BF16))
                 + _dot(e_n.astype(BF16), vn_ref[rs, hs]))
            o_ref[rs, hs] = (o * (1.0 / den)).astype(BF16)


def _attn_s_call(q, k, v, cache_k, cache_v, toep, l):
    nb, rows = SEQS_PER_STEP, SEQS_PER_STEP * DEC_SEQ
    first = ROWS_P // rows
    new = pl.BlockSpec((rows, ATTN_DIM), lambda i: (first + i, 0))
    cache = pl.BlockSpec((None, nb, ATTN_REACH * N_HEADS, HEAD_DIM), lambda i: (l, i, 0, 0))
    return _call(
        _attn_s_body, (DEC_BATCH // nb,),
        [new, new, new, cache, cache, _resident((None, N_HEADS, TOEP), (l, 0, 0))],
        pl.BlockSpec((rows, ATTN_DIM), lambda i: (i, 0)),
        jax.ShapeDtypeStruct((ROWS_S, ATTN_DIM), BF16),
        scratch=[pltpu.VMEM((N_HEADS, DEC_SEQ, ATTN_REACH + DEC_SEQ), F32)],
        name="attn_sample", vmem_mb=40)(q, k, v, cache_k, cache_v, toep)


def _toeplitz_table(rel_tab):
    left = 3 * QB - 1 - REL_CLIP
    right = TOEP - left - (2 * REL_CLIP + 1)
    pad = [(0, 0)] * (rel_tab.ndim - 1) + [(left, right)]
    return jnp.pad(rel_tab[..., ::-1] * LOG2E, pad, mode="edge").astype(F32)


def kernel(x_prompt, x_sample, p_prompt, p_sample, cache_conv, cache_k, cache_v, norm_mix, w_in, conv_w, conv_b, conv_ln_g, conv_ln_b, w_a_out, gmlp_ln_g, gmlp_ln_b, gmlp_ws, gmlp_bs, w_b_out, attn_rel_bias, w_c_out, w_o, norm_ffn, w_ff1, w_ff2, norm_ple, w_ple_gate, w_ple_proj, norm_final):
    p_p = p_prompt.reshape(DEPTH, ROWS_P, PLE_DIM)
    p_s = p_sample.reshape(DEPTH, ROWS_S, PLE_DIM)
    cache_k = cache_k.reshape(DEPTH, DEC_BATCH, ATTN_REACH * N_HEADS, HEAD_DIM)
    cache_v = cache_v.reshape(DEPTH, DEC_BATCH, ATTN_REACH * N_HEADS, HEAD_DIM)
    toep = _toeplitz_table(attn_rel_bias)

    vec = lambda x: x.reshape(x.shape[0], 1, x.shape[-1])
    g_mix = vec(jnp.concatenate([norm_mix, norm_final[None]], axis=0))
    conv_b3, conv_g3, conv_lb3 = vec(conv_b), vec(conv_ln_g), vec(conv_ln_b)
    gm_g3, gm_b3 = vec(gmlp_ln_g), vec(gmlp_ln_b)
    g_ffn3, g_ple3 = vec(norm_ffn), vec(norm_ple)

    w_ff1_b, w_ff2_b = w_ff1.astype(BF16), w_ff2.astype(BF16)

    outs = [[] for _ in range(7)]
    h, xn = _first_call(x_prompt.reshape(ROWS_P, D_MODEL), x_sample.reshape(ROWS_S, D_MODEL), g_mix)
    for l in range(DEPTH):
        q, k, v, k_rows, v_rows = _qkv_call(xn, w_in, l)
        gates = _gate_call(xn, w_in, l)

        ws = gmlp_ws[l]
        corner = jnp.tile(ws[:, :DEC_SEQ, :DEC_SEQ], (1, GMLP_CHUNK // DEC_SEQ, GMLP_CHUNK // DEC_SEQ))
        bias_p = jnp.repeat(gmlp_bs[l].T, GMLP_GC, axis=1)
        bias_s = jnp.tile(bias_p[:DEC_SEQ], (GMLP_CHUNK // DEC_SEQ, 1))
        a, yb_in, v_s = _glu_gmlp_call(xn, w_in, gm_g3, gm_b3, jnp.stack([ws, corner]),
                                       jnp.stack([bias_p, bias_s]), l)
        ya_in = _conv_call(a, cache_conv, conv_w, conv_b3, conv_g3, conv_lb3, l)

        o_p = _attn_p_call(q, k, v, toep, l)
        o_s = _attn_s_call(q, k, v, cache_k, cache_v, toep, l)

        h1, hn = _merge_call(ya_in, yb_in, o_p, o_s, gates, h, w_a_out, w_b_out, w_c_out, w_o, g_ffn3, l)
        h2 = _ffn_call(hn, h1, w_ff1_b, w_ff2_b, l)
        last = l == DEPTH - 1
        h, xn = _ple_call(h2, p_p, p_s, w_ple_gate, w_ple_proj, g_ple3, g_mix, l, last)

        outs[0].append(jnp.stack([a[(b + 1) * SEQ - CONV_HIST:(b + 1) * SEQ] for b in range(BATCH)]))
        outs[1].append(a[ROWS_P:].reshape(DEC_BATCH, DEC_SEQ, CONV_DIM)[:, DEC_SEQ - CONV_HIST:])
        for j, rows in enumerate((k_rows, v_rows)):
            rows = rows.reshape(KV_SLOTS, TM, N_HEADS, HEAD_DIM)
            outs[2 + j].append(rows[:BATCH])
            outs[4 + j].append(rows[BATCH].reshape(DEC_BATCH, DEC_SEQ, N_HEADS, HEAD_DIM))
        outs[6].append(v_s.reshape(DEC_BATCH, DEC_SEQ, GMLP_DIM))

    y_prompt = h.reshape(BATCH, SEQ, D_MODEL)
    y_sample = xn.reshape(DEC_BATCH, DEC_SEQ, D_MODEL)
    return (y_prompt, y_sample) + tuple(jnp.stack(x) for x in outs)
```

```python
import functools

import jax
import jax.numpy as jnp
from jax import lax
from jax.experimental import pallas as pl
from jax.experimental.pallas import tpu as pltpu

F32 = jnp.float32
BF16 = jnp.bfloat16

D_MODEL = 2048
BATCH = 2
SEQ = 4096
DEPTH = 4
DEC_BATCH = 16
DEC_SEQ = 32
CHUNK = 64
CONV_DIM = 1024
CONV_WIDTH = 31
CONV_HIST = CONV_WIDTH - 1
GMLP_DIM = 1024
GMLP_GROUPS = 8
GMLP_GC = GMLP_DIM // GMLP_GROUPS
GMLP_CHUNK = 128
N_HEADS = 8
HEAD_DIM = 128
ATTN_DIM = N_HEADS * HEAD_DIM
LEFT_CHUNKS = 8
ATTN_REACH = LEFT_CHUNKS * CHUNK
REL_CLIP = 128
D_FF = 4 * D_MODEL
PLE_DIM = 256
EPS = 1e-6
SCALE = HEAD_DIM ** -0.5
LOG2E = 1.4426950408889634
NEG = -1e30

ROWS_P = BATCH * SEQ
ROWS_S = DEC_BATCH * DEC_SEQ
ROWS = ROWS_P + ROWS_S
TM = 512
NT = ROWS // TM
NT_P = ROWS_P // TM
TILES_PER_SEQ = SEQ // TM
TM_MERGE = 256
TF = 2048
COLB = 1024
QB = 256
KB = 3 * QB
CONV_RC = 64
CONV_LB = 128
SUBLANES = 8
TOEP = 1024
KW = 8
KW_WIDE = 16
SEQS_PER_STEP = 2
MIB = 1 << 20


def _rms(x, g):
    return x * lax.rsqrt(jnp.mean(x * x, axis=-1, keepdims=True) + EPS) * g


def _ln(x, g, b):
    mu = jnp.mean(x, axis=-1, keepdims=True)
    xc = x - mu
    var = jnp.mean(xc * xc, axis=-1, keepdims=True)
    return xc * lax.rsqrt(var + EPS) * g + b


def _dot(a, b):
    return jnp.dot(a, b, preferred_element_type=F32)


def _dot_nt(a, b):
    return lax.dot_general(a, b, (((1,), (1,)), ((), ())), preferred_element_type=F32)


def _call(body, grid, in_specs, out_specs, out_shape, *, name, vmem_mb, scratch=()):
    return pl.pallas_call(
        body,
        grid=grid,
        in_specs=in_specs,
        out_specs=out_specs,
        out_shape=out_shape,
        scratch_shapes=list(scratch),
        compiler_params=pltpu.CompilerParams(
            dimension_semantics=("arbitrary",) * len(grid), vmem_limit_bytes=vmem_mb * MIB),
        name=name,
    )


def _rows(cols, tm=TM, cb=0, lag=0):
    return pl.BlockSpec((tm, cols), lambda s, *_: (jnp.maximum(s - lag, 0), cb))


def _resident(block, index):
    return pl.BlockSpec(block, lambda *_: index, pipeline_mode=pl.Buffered(1))


def _vec(l, n):
    return _resident((None, 1, n), (l, 0, 0))


def _wchunk(l, k, cols, kw, cb=0):
    return pl.BlockSpec((None, k // kw, cols), lambda s, *_: (l, jnp.minimum(s, kw - 1), cb))


def _wcopy(k, cols):
    return pltpu.VMEM((k, cols), BF16)


def _stage_weights(chunk_refs, w_refs, kw):
    s = pl.program_id(0)

    @pl.when(s < kw)
    def _():
        for c_ref, w_ref in zip(chunk_refs, w_refs):
            rows = c_ref.shape[0]
            w_ref[pl.ds(pl.multiple_of(s * rows, rows), rows), :] = c_ref[...].astype(BF16)

    return s >= kw


def _group_specs(cols):
    assert ROWS_S == TM
    return [pl.BlockSpec((TM, cols), lambda i: (jnp.minimum(i, NT_P - 1), 0)),
            pl.BlockSpec((TM, cols), lambda i: (0, 0))]


def _first_body(xp_ref, xs_ref, g_ref, h_ref, xn_ref):
    x = jnp.where(pl.program_id(0) == NT_P, xs_ref[...], xp_ref[...])
    h_ref[...] = x
    xn_ref[...] = _rms(x, g_ref[...]).astype(BF16)


def _first_call(xp, xs, g3):
    return _call(
        _first_body, (NT,), _group_specs(D_MODEL) + [_vec(0, D_MODEL)], [_rows(D_MODEL)] * 2,
        [jax.ShapeDtypeStruct((ROWS, D_MODEL), F32), jax.ShapeDtypeStruct((ROWS, D_MODEL), BF16)],
        name="rms_first", vmem_mb=40)(xp, xs, g3)


def _glu_gmlp_body(x_ref, ca_ref, cg_ref, cu_ref, cv_ref, g_ref, b_ref, ws_ref, bias_ref,
                   a_ref, yb_ref, vs_ref, wa_ref, wg_ref, wu_ref, wv_ref):
    main = _stage_weights((ca_ref, cg_ref, cu_ref, cv_ref), (wa_ref, wg_ref, wu_ref, wv_ref), KW_WIDE)

    @pl.when(main)
    def _():
        x = x_ref[...]
        v = _ln(jax.nn.gelu(_dot(x, wv_ref[...])), g_ref[...], b_ref[...])
        vs_ref[...] = v
        u = jax.nn.gelu(_dot(x, wu_ref[...]))
        a_ref[...] = _dot(x, wa_ref[...]) * jax.nn.sigmoid(_dot(x, wg_ref[...]))
        shift = jnp.where(pl.program_id(0) - KW_WIDE == NT_P, 5, 7)
        r = lax.broadcasted_iota(jnp.int32, (GMLP_CHUNK, GMLP_CHUNK), 0)
        c = lax.broadcasted_iota(jnp.int32, (GMLP_CHUNK, GMLP_CHUNK), 1)
        mask = (r >= c) & (lax.shift_right_logical(r, shift) == lax.shift_right_logical(c, shift))
        for g in range(GMLP_GROUPS):
            cs = slice(g * GMLP_GC, (g + 1) * GMLP_GC)
            wg = jnp.where(mask, ws_ref[g], 0.0).astype(BF16)
            bias = bias_ref[:, cs]
            for k in range(TM // GMLP_CHUNK):
                rs = slice(k * GMLP_CHUNK, (k + 1) * GMLP_CHUNK)
                mixed = _dot(wg, v[rs, cs].astype(BF16)) + bias
                yb_ref[rs, cs] = (u[rs, cs] * mixed).astype(BF16)


def _glu_gmlp_call(xn, w_in, ln_g, ln_b, wmix, bmix, l):
    assert ROWS_S == TM
    kw = KW_WIDE
    pick = lambda s: (jnp.where(s - kw == NT_P, 1, 0), 0, 0, 0)
    return _call(
        _glu_gmlp_body, (kw + NT,),
        [_rows(D_MODEL, lag=kw)] + [_wchunk(l, D_MODEL, COLB, kw, j) for j in range(4)]
        + [_vec(l, GMLP_DIM), _vec(l, GMLP_DIM),
           pl.BlockSpec((None, GMLP_GROUPS, GMLP_CHUNK, GMLP_CHUNK), pick),
           pl.BlockSpec((None, GMLP_CHUNK, GMLP_DIM), lambda s: pick(s)[:3])],
        [_rows(CONV_DIM, lag=kw), _rows(GMLP_DIM, lag=kw), pl.BlockSpec((TM, GMLP_DIM), lambda s: (0, 0))],
        [jax.ShapeDtypeStruct((ROWS, CONV_DIM), F32), jax.ShapeDtypeStruct((ROWS, GMLP_DIM), BF16),
         jax.ShapeDtypeStruct((ROWS_S, GMLP_DIM), F32)],
        scratch=[_wcopy(D_MODEL, COLB)] * 4,
        name="inproj_glu_gmlp", vmem_mb=58)(xn, w_in, w_in, w_in, w_in, ln_g, ln_b, wmix, bmix)


KV_SLOTS = BATCH + 1


def _qkv_body(x_ref, cq_ref, ck_ref, cv_ref, q_ref, k_ref, v_ref, kr_ref, vr_ref, wq_ref, wk_ref, wv_ref):
    @pl.when(_stage_weights((cq_ref, ck_ref, cv_ref), (wq_ref, wk_ref, wv_ref), KW))
    def _():
        x = x_ref[...]
        q_ref[...] = (_dot(x, wq_ref[...]) * (SCALE * LOG2E)).astype(BF16)
        for w_ref, o_ref, rows_ref in ((wk_ref, k_ref, kr_ref), (wv_ref, v_ref, vr_ref)):
            y = _dot(x, w_ref[...])
            o_ref[...] = y.astype(BF16)
            for h in range(N_HEADS):
                rows_ref[pl.ds(h, TM, stride=N_HEADS), :] = y[:, h * HEAD_DIM:(h + 1) * HEAD_DIM]


def _qkv_call(xn, w_in, l):
    act = jax.ShapeDtypeStruct((ROWS, ATTN_DIM), BF16)
    rows = jax.ShapeDtypeStruct((KV_SLOTS * TM * N_HEADS, HEAD_DIM), F32)
    kv_slot = lambda s: (jnp.maximum(s - KW, 0) // TILES_PER_SEQ, 0)
    return _call(
        _qkv_body, (KW + NT,),
        [_rows(D_MODEL, lag=KW)] + [_wchunk(l, D_MODEL, COLB, KW, 4 + j) for j in range(3)],
        [_rows(ATTN_DIM, lag=KW)] * 3 + [pl.BlockSpec((TM * N_HEADS, HEAD_DIM), kv_slot)] * 2,
        [act, act, act, rows, rows],
        scratch=[_wcopy(D_MODEL, COLB)] * 3,
        name="inproj_qkv", vmem_mb=52)(xn, w_in, w_in, w_in)


N_GATE_BLOCKS = 3 * D_MODEL // COLB


def _gate_body(x_ref, *refs):
    n = N_GATE_BLOCKS
    c_refs, o_ref, w_refs = refs[:n], refs[n], refs[n + 1:]

    @pl.when(_stage_weights(c_refs, w_refs, KW_WIDE))
    def _():
        x = x_ref[...]
        for j, w_ref in enumerate(w_refs):
            o_ref[:, j * COLB:(j + 1) * COLB] = jax.nn.sigmoid(_dot(x, w_ref[...])).astype(BF16)


def _gate_call(xn, w_in, l):
    n = N_GATE_BLOCKS
    return _call(
        _gate_body, (KW_WIDE + NT,),
        [_rows(D_MODEL, lag=KW_WIDE)] + [_wchunk(l, D_MODEL, COLB, KW_WIDE, 7 + j) for j in range(n)],
        _rows(3 * D_MODEL, lag=KW_WIDE), jax.ShapeDtypeStruct((ROWS, 3 * D_MODEL), BF16),
        scratch=[_wcopy(D_MODEL, COLB)] * n,
        name="inproj_gates", vmem_mb=56)(xn, *([w_in] * n))


def _merge_body(a_ref, b_ref, cp_ref, cs_ref, g0_ref, g1_ref, g2_ref, h_ref,
                ca_ref, cb_ref, cc_ref, co_ref, gf_ref, h1_ref, hn_ref,
                wa_ref, wb_ref, wc_ref, wo_ref):
    main = _stage_weights((ca_ref, cb_ref, cc_ref, co_ref), (wa_ref, wb_ref, wc_ref, wo_ref), KW_WIDE)

    @pl.when(main)
    def _():
        is_sample = pl.program_id(0) - KW_WIDE >= ROWS_P // TM_MERGE
        c = jnp.where(is_sample, cs_ref[...], cp_ref[...])
        m = g0_ref[...].astype(F32) * _dot(a_ref[...], wa_ref[...])
        m = m + g1_ref[...].astype(F32) * _dot(b_ref[...], wb_ref[...])
        m = m + g2_ref[...].astype(F32) * _dot(c, wc_ref[...])
        h1 = h_ref[...] + _dot(m.astype(BF16), wo_ref[...])
        h1_ref[...] = h1
        hn_ref[...] = _rms(h1, gf_ref[...]).astype(BF16)


def _merge_call(a, b, c_p, c_s, gates, h, w_a, w_b, w_c, w_o, g_ffn, l):
    tm, kw = TM_MERGE, KW_WIDE
    np_ = ROWS_P // tm
    rows = functools.partial(_rows, tm=tm, lag=kw)
    return _call(
        _merge_body, (kw + ROWS // tm,),
        [rows(CONV_DIM), rows(GMLP_DIM),
         pl.BlockSpec((tm, ATTN_DIM), lambda s: (jnp.clip(s - kw, 0, np_ - 1), 0)),
         pl.BlockSpec((tm, ATTN_DIM), lambda s: (jnp.maximum(s - kw - np_, 0), 0)),
         rows(D_MODEL, cb=0), rows(D_MODEL, cb=1), rows(D_MODEL, cb=2), rows(D_MODEL),
         _wchunk(l, CONV_DIM, D_MODEL, kw), _wchunk(l, GMLP_DIM, D_MODEL, kw),
         _wchunk(l, ATTN_DIM, D_MODEL, kw), _wchunk(l, D_MODEL, D_MODEL, kw), _vec(l, D_MODEL)],
        [rows(D_MODEL), rows(D_MODEL)],
        [jax.ShapeDtypeStruct((ROWS, D_MODEL), F32), jax.ShapeDtypeStruct((ROWS, D_MODEL), BF16)],
        scratch=[_wcopy(CONV_DIM, D_MODEL), _wcopy(GMLP_DIM, D_MODEL), _wcopy(ATTN_DIM, D_MODEL),
                 _wcopy(D_MODEL, D_MODEL)],
        name="merge_out", vmem_mb=58)(a, b, c_p, c_s, gates, gates, gates, h, w_a, w_b, w_c, w_o, g_ffn)


def _ffn_body(hn_ref, w1_ref, w2_ref, h1_ref, o_ref):
    def delta():
        a = jnp.square(jnp.maximum(_dot(hn_ref[...], w1_ref[...]), 0.0)).astype(BF16)
        return _dot(a, w2_ref[...])

    @pl.when(pl.program_id(1) == 0)
    def _():
        o_ref[...] = h1_ref[...] + delta()

    @pl.when(pl.program_id(1) != 0)
    def _():
        o_ref[...] += delta()


def _ffn_call(hn, h1, w1, w2, l):
    return _call(
        _ffn_body, (NT, D_FF // TF),
        [pl.BlockSpec((TM, D_MODEL), lambda i, f: (i, 0)),
         pl.BlockSpec((None, D_MODEL, TF), lambda i, f: (l, 0, f)),
         pl.BlockSpec((None, TF, D_MODEL), lambda i, f: (l, f, 0)),
         pl.BlockSpec((TM, D_MODEL), lambda i, f: (i, 0))],
        pl.BlockSpec((TM, D_MODEL), lambda i, f: (i, 0)),
        jax.ShapeDtypeStruct((ROWS, D_MODEL), F32),
        name="ffn", vmem_mb=62)(hn, w1, w2, h1)


def _ple_update(tile, h2_ref, pp_ref, ps_ref, wg_ref, wp_ref, gp_ref):
    h2 = h2_ref[...]
    p = jnp.where(tile == NT_P, ps_ref[...], pp_ref[...])
    gate = jax.nn.sigmoid(_dot(_rms(h2, gp_ref[...]).astype(BF16), wg_ref[...]))
    return h2 + gate * _dot(p.astype(BF16), wp_ref[...])


def _ple_body(h2_ref, pp_ref, ps_ref, cg_ref, cp_ref, gp_ref, gn_ref, h3_ref, xn_ref, wg_ref, wp_ref):
    @pl.when(_stage_weights((cg_ref, cp_ref), (wg_ref, wp_ref), KW))
    def _():
        h3 = _ple_update(pl.program_id(0) - KW, h2_ref, pp_ref, ps_ref, wg_ref, wp_ref, gp_ref)
        h3_ref[...] = h3
        xn_ref[...] = _rms(h3, gn_ref[...]).astype(BF16)


def _ple_last_body(h2_ref, pp_ref, ps_ref, cg_ref, cp_ref, gp_ref, gn_ref, yp_ref, ys_ref, wg_ref, wp_ref):
    main = _stage_weights((cg_ref, cp_ref), (wg_ref, wp_ref), KW)
    tile = pl.program_id(0) - KW

    def final():
        return _rms(_ple_update(tile, h2_ref, pp_ref, ps_ref, wg_ref, wp_ref, gp_ref), gn_ref[...])

    @pl.when(main & (tile < NT_P))
    def _():
        yp_ref[...] = final()

    @pl.when(tile >= NT_P)
    def _():
        ys_ref[...] = final()


def _ple_call(h2, p_p, p_s, w_g, w_p, g_ple, g_next, l, last):
    assert ROWS_S == TM
    ins = [_rows(D_MODEL, lag=KW),
           pl.BlockSpec((None, TM, PLE_DIM), lambda s: (l, jnp.clip(s - KW, 0, NT_P - 1), 0)),
           pl.BlockSpec((None, TM, PLE_DIM), lambda s: (l, 0, 0)),
           _wchunk(l, D_MODEL, D_MODEL, KW), _wchunk(l, PLE_DIM, D_MODEL, KW),
           _vec(l, D_MODEL), _vec(l + 1, D_MODEL)]
    scratch = [_wcopy(D_MODEL, D_MODEL), _wcopy(PLE_DIM, D_MODEL)]
    args = (h2, p_p, p_s, w_g, w_p, g_ple, g_next)
    if not last:
        return _call(
            _ple_body, (KW + NT,), ins, [_rows(D_MODEL, lag=KW)] * 2,
            [jax.ShapeDtypeStruct((ROWS, D_MODEL), F32), jax.ShapeDtypeStruct((ROWS, D_MODEL), BF16)],
            scratch=scratch, name="ple", vmem_mb=56)(*args)
    return _call(
        _ple_last_body, (KW + NT,), ins,
        [pl.BlockSpec((TM, D_MODEL), lambda s: (jnp.clip(s - KW, 0, NT_P - 1), 0)),
         pl.BlockSpec((TM, D_MODEL), lambda s: (0, 0))],
        [jax.ShapeDtypeStruct((ROWS_P, D_MODEL), F32), jax.ShapeDtypeStruct((ROWS_S, D_MODEL), F32)],
        scratch=scratch, name="ple_final", vmem_mb=56)(*args)


def _conv_rows(ext_ref, y_ref, cw_ref, ext_base, y_base, rc):
    for lb in range(CONV_DIM // CONV_LB):
        ls = slice(lb * CONV_LB, (lb + 1) * CONV_LB)
        acc = None
        for r in range(SUBLANES):
            rows = rc + (SUBLANES if r else 0)
            u = None
            for j in range(CONV_WIDTH):
                if (2 + j) % SUBLANES == r:
                    start = ext_base + (2 + j - r)
                    if not isinstance(start, int):
                        start = pl.multiple_of(start, SUBLANES)
                    term = cw_ref[j:j + 1, ls] * ext_ref[pl.ds(start, rows), ls]
                    u = term if u is None else u + term
            part = u[r:r + rc, :]
            acc = part if acc is None else acc + part
        y_ref[pl.ds(y_base, rc), ls] = acc


def _conv_body(a_ref, cache_ref, cw_ref, cb_ref, g_ref, b_ref, o_ref, ext_ref, y_ref):
    i = pl.program_id(0)
    hist = 32

    @pl.when(i < NT_P)
    def _():
        @pl.when(i % TILES_PER_SEQ == 0)
        def _():
            ext_ref[0:hist, :] = jnp.zeros((hist, CONV_DIM), F32)

        @pl.when(i % TILES_PER_SEQ != 0)
        def _():
            ext_ref[0:hist, :] = ext_ref[TM:TM + hist, :]

        ext_ref[hist:hist + TM, :] = a_ref[...]

        def chunk(c, carry):
            r0 = pl.multiple_of(c * CONV_RC, CONV_RC)
            _conv_rows(ext_ref, y_ref, cw_ref, r0, r0, CONV_RC)
            return carry

        lax.fori_loop(0, TM // CONV_RC, chunk, 0)

    @pl.when(i == NT_P)
    def _():
        def seq(s, carry):
            r0 = pl.multiple_of(s * DEC_SEQ, DEC_SEQ)
            ext_ref[hist - CONV_HIST:hist, :] = cache_ref[s]
            ext_ref[hist:hist + DEC_SEQ, :] = a_ref[pl.ds(r0, DEC_SEQ), :]
            _conv_rows(ext_ref, y_ref, cw_ref, 0, r0, DEC_SEQ)
            return carry

        lax.fori_loop(0, DEC_BATCH, seq, 0)

    y = _ln(y_ref[...] + cb_ref[...], g_ref[...], b_ref[...])
    o_ref[...] = (y * jax.nn.sigmoid(y)).astype(BF16)


def _conv_call(a, cache_conv, conv_w, conv_b, ln_g, ln_b, l):
    return _call(
        _conv_body, (NT,),
        [_rows(CONV_DIM),
         _resident((None, DEC_BATCH, CONV_HIST, CONV_DIM), (l, 0, 0, 0)),
         _resident((None, CONV_WIDTH, CONV_DIM), (l, 0, 0)),
         _vec(l, CONV_DIM), _vec(l, CONV_DIM), _vec(l, CONV_DIM)],
        _rows(CONV_DIM), jax.ShapeDtypeStruct((ROWS, CONV_DIM), BF16),
        scratch=[pltpu.VMEM((TM + 32, CONV_DIM), F32), pltpu.VMEM((TM, CONV_DIM), F32)],
        name="conv_mixer", vmem_mb=32)(a, cache_conv, conv_w, conv_b, ln_g, ln_b)


def _toeplitz_rows(tab_ref, h, rows):
    t = jnp.broadcast_to(tab_ref[h:h + 1, :], (rows, TOEP))
    return pltpu.roll(t, TOEP - QB + 1, 1, stride=1, stride_axis=0)


def _attn_p_body(q_ref, k0_ref, k1_ref, k2_ref, v0_ref, v1_ref, v2_ref, tab_ref, o_ref, bias_ref):
    m = pl.program_id(1)
    k_refs = (k0_ref, k1_ref, k2_ref)
    v_refs = (v0_ref, v1_ref, v2_ref)

    @pl.when((pl.program_id(0) == 0) & (m == 0))
    def _():
        r = lax.broadcasted_iota(jnp.int32, (QB, KB), 0) // CHUNK
        j = lax.broadcasted_iota(jnp.int32, (QB, KB), 1) // CHUNK
        band = (j >= r) & (j <= r + LEFT_CHUNKS)
        for h in range(N_HEADS):
            bias = jnp.where(band, _toeplitz_rows(tab_ref, h, QB)[:, :KB], NEG)
            for s in range(3):
                bias_ref[h, s] = bias[:, s * QB:(s + 1) * QB]
            bias_ref[h, 3] = jnp.full((QB, QB), NEG, F32)

    slot = [jnp.where(m - 2 + s >= 0, s, 3) for s in range(3)]
    for h in range(N_HEADS):
        hs = slice(h * HEAD_DIM, (h + 1) * HEAD_DIM)
        qh = q_ref[:, hs]
        sc = [_dot_nt(qh, k_refs[s][:, hs]) + bias_ref[h, slot[s]] for s in range(3)]
        mx = jnp.max(functools.reduce(jnp.maximum, sc), axis=-1, keepdims=True)
        e = [jnp.exp2(x - mx) for x in sc]
        den = jnp.sum(functools.reduce(jnp.add, e), axis=-1, keepdims=True)
        o = functools.reduce(jnp.add, [_dot(e[s].astype(BF16), v_refs[s][:, hs]) for s in range(3)])
        o_ref[:, hs] = (o * (1.0 / den)).astype(BF16)


def _attn_p_call(q, k, v, toep, l):
    nq = SEQ // QB

    def kv_spec(s):
        return pl.BlockSpec((QB, ATTN_DIM), lambda b, m: (b * nq + jnp.maximum(m - 2 + s, 0), 0))

    return _call(
        _attn_p_body, (BATCH, nq),
        [pl.BlockSpec((QB, ATTN_DIM), lambda b, m: (b * nq + m, 0))]
        + [kv_spec(s) for s in range(3)] + [kv_spec(s) for s in range(3)]
        + [_resident((None, N_HEADS, TOEP), (l, 0, 0))],
        pl.BlockSpec((QB, ATTN_DIM), lambda b, m: (b * nq + m, 0)),
        jax.ShapeDtypeStruct((ROWS_P, ATTN_DIM), BF16),
        scratch=[pltpu.VMEM((N_HEADS, 4, QB, QB), F32)],
        name="attn_prompt", vmem_mb=40)(q, k, k, k, v, v, v, toep)


def _attn_s_body(q_ref, kn_ref, vn_ref, kc_ref, vc_ref, tab_ref, o_ref, bias_ref):
    w = ATTN_REACH

    @pl.when(pl.program_id(0) == 0)
    def _():
        for h in range(N_HEADS):
            bias_ref[h] = _toeplitz_rows(tab_ref, h, DEC_SEQ)[:, :w + DEC_SEQ]

    for b in range(SEQS_PER_STEP):
        rs = slice(b * DEC_SEQ, (b + 1) * DEC_SEQ)
        for h in range(N_HEADS):
            hs = slice(h * HEAD_DIM, (h + 1) * HEAD_DIM)
            rows_h = pl.ds(h, w, stride=N_HEADS)
            qh = q_ref[rs, hs]
            s_c = _dot_nt(qh, kc_ref[b, rows_h, :].astype(BF16)) + bias_ref[h, :, :w]
            s_n = _dot_nt(qh, kn_ref[rs, hs]) + bias_ref[h, :, w:]
            mx = jnp.maximum(jnp.max(s_c, axis=-1, keepdims=True), jnp.max(s_n, axis=-1, keepdims=True))
            e_c = jnp.exp2(s_c - mx)
            e_n = jnp.exp2(s_n - mx)
            den = jnp.sum(e_c, axis=-1, keepdims=True) + jnp.sum(e_n, axis=-1, keepdims=True)
            o = (_dot(e_c.astype(BF16), vc_ref[b, rows_h, :].astype(BF16))
                 + _dot(e_n.astype(BF16), vn_ref[rs, hs]))
            o_ref[rs, hs] = (o * (1.0 / den)).astype(BF16)


def _attn_s_call(q, k, v, cache_k, cache_v, toep, l):
    nb, rows = SEQS_PER_STEP, SEQS_PER_STEP * DEC_SEQ
    first = ROWS_P // rows
    new = pl.BlockSpec((rows, ATTN_DIM), lambda i: (first + i, 0))
    cache = pl.BlockSpec((None, nb, ATTN_REACH * N_HEADS, HEAD_DIM), lambda i: (l, i, 0, 0))
    return _call(
        _attn_s_body, (DEC_BATCH // nb,),
        [new, new, new, cache, cache, _resident((None, N_HEADS, TOEP), (l, 0, 0))],
        pl.BlockSpec((rows, ATTN_DIM), lambda i: (i, 0)),
        jax.ShapeDtypeStruct((ROWS_S, ATTN_DIM), BF16),
        scratch=[pltpu.VMEM((N_HEADS, DEC_SEQ, ATTN_REACH + DEC_SEQ), F32)],
        name="attn_sample", vmem_mb=40)(q, k, v, cache_k, cache_v, toep)


def _toeplitz_table(rel_tab):
    left = 3 * QB - 1 - REL_CLIP
    right = TOEP - left - (2 * REL_CLIP + 1)
    pad = [(0, 0)] * (rel_tab.ndim - 1) + [(left, right)]
    return jnp.pad(rel_tab[..., ::-1] * LOG2E, pad, mode="edge").astype(F32)


def kernel(x_prompt, x_sample, p_prompt, p_sample, cache_conv, cache_k, cache_v, norm_mix, w_in, conv_w, conv_b, conv_ln_g, conv_ln_b, w_a_out, gmlp_ln_g, gmlp_ln_b, gmlp_ws, gmlp_bs, w_b_out, attn_rel_bias, w_c_out, w_o, norm_ffn, w_ff1, w_ff2, norm_ple, w_ple_gate, w_ple_proj, norm_final):
    p_p = p_prompt.reshape(DEPTH, ROWS_P, PLE_DIM)
    p_s = p_sample.reshape(DEPTH, ROWS_S, PLE_DIM)
    cache_k = cache_k.reshape(DEPTH, DEC_BATCH, ATTN_REACH * N_HEADS, HEAD_DIM)
    cache_v = cache_v.reshape(DEPTH, DEC_BATCH, ATTN_REACH * N_HEADS, HEAD_DIM)
    toep = _toeplitz_table(attn_rel_bias)

    vec = lambda x: x.reshape(x.shape[0], 1, x.shape[-1])
    g_mix = vec(jnp.concatenate([norm_mix, norm_final[None]], axis=0))
    conv_b3, conv_g3, conv_lb3 = vec(conv_b), vec(conv_ln_g), vec(conv_ln_b)
    gm_g3, gm_b3 = vec(gmlp_ln_g), vec(gmlp_ln_b)
    g_ffn3, g_ple3 = vec(norm_ffn), vec(norm_ple)

    w_ff1_b, w_ff2_b = w_ff1.astype(BF16), w_ff2.astype(BF16)

    outs = [[] for _ in range(7)]
    h, xn = _first_call(x_prompt.reshape(ROWS_P, D_MODEL), x_sample.reshape(ROWS_S, D_MODEL), g_mix)
    for l in range(DEPTH):
        q, k, v, k_rows, v_rows = _qkv_call(xn, w_in, l)
        gates = _gate_call(xn, w_in, l)

        ws = gmlp_ws[l]
        corner = jnp.tile(ws[:, :DEC_SEQ, :DEC_SEQ], (1, GMLP_CHUNK // DEC_SEQ, GMLP_CHUNK // DEC_SEQ))
        bias_p = jnp.repeat(gmlp_bs[l].T, GMLP_GC, axis=1)
        bias_s = jnp.tile(bias_p[:DEC_SEQ], (GMLP_CHUNK // DEC_SEQ, 1))
        a, yb_in, v_s = _glu_gmlp_call(xn, w_in, gm_g3, gm_b3, jnp.stack([ws, corner]),
                                       jnp.stack([bias_p, bias_s]), l)
        ya_in = _conv_call(a, cache_conv, conv_w, conv_b3, conv_g3, conv_lb3, l)

        o_p = _attn_p_call(q, k, v, toep, l)
        o_s = _attn_s_call(q, k, v, cache_k, cache_v, toep, l)

        h1, hn = _merge_call(ya_in, yb_in, o_p, o_s, gates, h, w_a_out, w_b_out, w_c_out, w_o, g_ffn3, l)
        h2 = _ffn_call(hn, h1, w_ff1_b, w_ff2_b, l)
        last = l == DEPTH - 1
        h, xn = _ple_call(h2, p_p, p_s, w_ple_gate, w_ple_proj, g_ple3, g_mix, l, last)

        outs[0].append(jnp.stack([a[(b + 1) * SEQ - CONV_HIST:(b + 1) * SEQ] for b in range(BATCH)]))
        outs[1].append(a[ROWS_P:].reshape(DEC_BATCH, DEC_SEQ, CONV_DIM)[:, DEC_SEQ - CONV_HIST:])
        for j, rows in enumerate((k_rows, v_rows)):
            rows = rows.reshape(KV_SLOTS, TM, N_HEADS, HEAD_DIM)
            outs[2 + j].append(rows[:BATCH])
            outs[4 + j].append(rows[BATCH].reshape(DEC_BATCH, DEC_SEQ, N_HEADS, HEAD_DIM))
        outs[6].append(v_s.reshape(DEC_BATCH, DEC_SEQ, GMLP_DIM))

    y_prompt = h.reshape(BATCH, SEQ, D_MODEL)
    y_sample = xn.reshape(DEC_BATCH, DEC_SEQ, D_MODEL)
    return (y_prompt, y_sample) + tuple(jnp.stack(x) for x in outs)
```

```python
import functools

import jax
import jax.numpy as jnp
from jax import lax
from jax.experimental import pallas as pl
from jax.experimental.pallas import tpu as pltpu

F32 = jnp.float32
BF16 = jnp.bfloat16

D_MODEL = 2048
BATCH = 2
SEQ = 4096
DEPTH = 4
DEC_BATCH = 16
DEC_SEQ = 32
CHUNK = 64
CONV_DIM = 1024
CONV_WIDTH = 31
CONV_HIST = CONV_WIDTH - 1
GMLP_DIM = 1024
GMLP_GROUPS = 8
GMLP_GC = GMLP_DIM // GMLP_GROUPS
GMLP_CHUNK = 128
N_HEADS = 8
HEAD_DIM = 128
ATTN_DIM = N_HEADS * HEAD_DIM
LEFT_CHUNKS = 8
ATTN_REACH = LEFT_CHUNKS * CHUNK
REL_CLIP = 128
D_FF = 4 * D_MODEL
PLE_DIM = 256
EPS = 1e-6
SCALE = HEAD_DIM ** -0.5
LOG2E = 1.4426950408889634
NEG = -1e30

ROWS_P = BATCH * SEQ
ROWS_S = DEC_BATCH * DEC_SEQ
ROWS = ROWS_P + ROWS_S
TM = 512
NT = ROWS // TM
NT_P = ROWS_P // TM
TILES_PER_SEQ = SEQ // TM
TM_MERGE = 256
TF = 2048
COLB = 1024
QB = 256
KB = 3 * QB
QPS = 2
CONV_RC = 64
CONV_LB = 128
SUBLANES = 8
TOEP = 1024
KW = 8
KW_WIDE = 8
SEQS_PER_STEP = 2
MIB = 1 << 20


def _rms(x, g):
    return x * lax.rsqrt(jnp.mean(x * x, axis=-1, keepdims=True) + EPS) * g


def _ln(x, g, b):
    mu = jnp.mean(x, axis=-1, keepdims=True)
    xc = x - mu
    var = jnp.mean(xc * xc, axis=-1, keepdims=True)
    return xc * lax.rsqrt(var + EPS) * g + b


def _dot(a, b):
    return jnp.dot(a, b, preferred_element_type=F32)


def _dot_nt(a, b):
    return lax.dot_general(a, b, (((1,), (1,)), ((), ())), preferred_element_type=F32)


def _call(body, grid, in_specs, out_specs, out_shape, *, name, vmem_mb, scratch=()):
    return pl.pallas_call(
        body,
        grid=grid,
        in_specs=in_specs,
        out_specs=out_specs,
        out_shape=out_shape,
        scratch_shapes=list(scratch),
        compiler_params=pltpu.CompilerParams(
            dimension_semantics=("arbitrary",) * len(grid), vmem_limit_bytes=vmem_mb * MIB),
        name=name,
    )


def _rows(cols, tm=TM, cb=0, lag=0):
    return pl.BlockSpec((tm, cols), lambda s, *_: (jnp.maximum(s - lag, 0), cb))


def _resident(block, index):
    return pl.BlockSpec(block, lambda *_: index, pipeline_mode=pl.Buffered(1))


def _vec(l, n):
    return _resident((None, 1, n), (l, 0, 0))


def _wchunk(l, k, cols, kw, cb=0):
    return pl.BlockSpec((None, k // kw, cols), lambda s, *_: (l, jnp.minimum(s, kw - 1), cb))


def _wcopy(k, cols):
    return pltpu.VMEM((k, cols), BF16)


def _stage_weights(chunk_refs, w_refs, kw):
    s = pl.program_id(0)

    @pl.when(s < kw)
    def _():
        for c_ref, w_ref in zip(chunk_refs, w_refs):
            rows = c_ref.shape[0]
            w_ref[pl.ds(pl.multiple_of(s * rows, rows), rows), :] = c_ref[...].astype(BF16)

    return s >= kw


def _group_specs(cols):
    assert ROWS_S == TM
    return [pl.BlockSpec((TM, cols), lambda i: (jnp.minimum(i, NT_P - 1), 0)),
            pl.BlockSpec((TM, cols), lambda i: (0, 0))]


def _first_body(xp_ref, xs_ref, g_ref, h_ref, xn_ref):
    x = jnp.where(pl.program_id(0) == NT_P, xs_ref[...], xp_ref[...])
    h_ref[...] = x
    xn_ref[...] = _rms(x, g_ref[...]).astype(BF16)


def _first_call(xp, xs, g3):
    return _call(
        _first_body, (NT,), _group_specs(D_MODEL) + [_vec(0, D_MODEL)], [_rows(D_MODEL)] * 2,
        [jax.ShapeDtypeStruct((ROWS, D_MODEL), F32), jax.ShapeDtypeStruct((ROWS, D_MODEL), BF16)],
        name="rms_first", vmem_mb=40)(xp, xs, g3)


def _glu_gmlp_body(x_ref, ca_ref, cg_ref, cu_ref, cv_ref, g_ref, b_ref, ws_ref, bias_ref,
                   a_ref, yb_ref, vs_ref, wa_ref, wg_ref, wu_ref, wv_ref):
    main = _stage_weights((ca_ref, cg_ref, cu_ref, cv_ref), (wa_ref, wg_ref, wu_ref, wv_ref), KW_WIDE)

    @pl.when(main)
    def _():
        x = x_ref[...]
        v = _ln(jax.nn.gelu(_dot(x, wv_ref[...])), g_ref[...], b_ref[...])
        vs_ref[...] = v
        u = jax.nn.gelu(_dot(x, wu_ref[...]))
        a_ref[...] = _dot(x, wa_ref[...]) * jax.nn.sigmoid(_dot(x, wg_ref[...]))
        shift = jnp.where(pl.program_id(0) - KW_WIDE == NT_P, 5, 7)
        r = lax.broadcasted_iota(jnp.int32, (GMLP_CHUNK, GMLP_CHUNK), 0)
        c = lax.broadcasted_iota(jnp.int32, (GMLP_CHUNK, GMLP_CHUNK), 1)
        mask = (r >= c) & (lax.shift_right_logical(r, shift) == lax.shift_right_logical(c, shift))
        for g in range(GMLP_GROUPS):
            cs = slice(g * GMLP_GC, (g + 1) * GMLP_GC)
            wg = jnp.where(mask, ws_ref[g], 0.0).astype(BF16)
            bias = bias_ref[:, cs]
            for k in range(TM // GMLP_CHUNK):
                rs = slice(k * GMLP_CHUNK, (k + 1) * GMLP_CHUNK)
                mixed = _dot(wg, v[rs, cs].astype(BF16)) + bias
                yb_ref[rs, cs] = (u[rs, cs] * mixed).astype(BF16)


def _glu_gmlp_call(xn, w_in, ln_g, ln_b, wmix, bmix, l):
    assert ROWS_S == TM
    kw = KW_WIDE
    pick = lambda s: (jnp.where(s - kw == NT_P, 1, 0), 0, 0, 0)
    return _call(
        _glu_gmlp_body, (kw + NT,),
        [_rows(D_MODEL, lag=kw)] + [_wchunk(l, D_MODEL, COLB, kw, j) for j in range(4)]
        + [_vec(l, GMLP_DIM), _vec(l, GMLP_DIM),
           pl.BlockSpec((None, GMLP_GROUPS, GMLP_CHUNK, GMLP_CHUNK), pick),
           pl.BlockSpec((None, GMLP_CHUNK, GMLP_DIM), lambda s: pick(s)[:3])],
        [_rows(CONV_DIM, lag=kw), _rows(GMLP_DIM, lag=kw), pl.BlockSpec((TM, GMLP_DIM), lambda s: (0, 0))],
        [jax.ShapeDtypeStruct((ROWS, CONV_DIM), F32), jax.ShapeDtypeStruct((ROWS, GMLP_DIM), BF16),
         jax.ShapeDtypeStruct((ROWS_S, GMLP_DIM), F32)],
        scratch=[_wcopy(D_MODEL, COLB)] * 4,
        name="inproj_glu_gmlp", vmem_mb=58)(xn, w_in, w_in, w_in, w_in, ln_g, ln_b, wmix, bmix)


KV_SLOTS = BATCH + 1


def _qkv_body(x_ref, cq_ref, ck_ref, cv_ref, q_ref, k_ref, v_ref, kr_ref, vr_ref, wq_ref, wk_ref, wv_ref):
    @pl.when(_stage_weights((cq_ref, ck_ref, cv_ref), (wq_ref, wk_ref, wv_ref), KW))
    def _():
        x = x_ref[...]
        q_ref[...] = (_dot(x, wq_ref[...]) * (SCALE * LOG2E)).astype(BF16)
        for w_ref, o_ref, rows_ref in ((wk_ref, k_ref, kr_ref), (wv_ref, v_ref, vr_ref)):
            y = _dot(x, w_ref[...])
            o_ref[...] = y.astype(BF16)
            for h in range(N_HEADS):
                rows_ref[pl.ds(h, TM, stride=N_HEADS), :] = y[:, h * HEAD_DIM:(h + 1) * HEAD_DIM]


def _qkv_call(xn, w_in, l):
    act = jax.ShapeDtypeStruct((ROWS, ATTN_DIM), BF16)
    rows = jax.ShapeDtypeStruct((KV_SLOTS * TM * N_HEADS, HEAD_DIM), F32)
    kv_slot = lambda s: (jnp.maximum(s - KW, 0) // TILES_PER_SEQ, 0)
    return _call(
        _qkv_body, (KW + NT,),
        [_rows(D_MODEL, lag=KW)] + [_wchunk(l, D_MODEL, COLB, KW, 4 + j) for j in range(3)],
        [_rows(ATTN_DIM, lag=KW)] * 3 + [pl.BlockSpec((TM * N_HEADS, HEAD_DIM), kv_slot)] * 2,
        [act, act, act, rows, rows],
        scratch=[_wcopy(D_MODEL, COLB)] * 3,
        name="inproj_qkv", vmem_mb=52)(xn, w_in, w_in, w_in)


N_GATE_BLOCKS = 3 * D_MODEL // COLB


def _gate_body(x_ref, *refs):
    n = N_GATE_BLOCKS
    c_refs, o_ref, w_refs = refs[:n], refs[n], refs[n + 1:]

    @pl.when(_stage_weights(c_refs, w_refs, KW_WIDE))
    def _():
        x = x_ref[...]
        for j, w_ref in enumerate(w_refs):
            o_ref[:, j * COLB:(j + 1) * COLB] = jax.nn.sigmoid(_dot(x, w_ref[...])).astype(BF16)


def _gate_call(xn, w_in, l):
    n = N_GATE_BLOCKS
    return _call(
        _gate_body, (KW_WIDE + NT,),
        [_rows(D_MODEL, lag=KW_WIDE)] + [_wchunk(l, D_MODEL, COLB, KW_WIDE, 7 + j) for j in range(n)],
        _rows(3 * D_MODEL, lag=KW_WIDE), jax.ShapeDtypeStruct((ROWS, 3 * D_MODEL), BF16),
        scratch=[_wcopy(D_MODEL, COLB)] * n,
        name="inproj_gates", vmem_mb=56)(xn, *([w_in] * n))


def _merge_body(a_ref, b_ref, cp_ref, cs_ref, g0_ref, g1_ref, g2_ref, h_ref,
                ca_ref, cb_ref, cc_ref, co_ref, gf_ref, h1_ref, hn_ref,
                wa_ref, wb_ref, wc_ref, wo_ref):
    main = _stage_weights((ca_ref, cb_ref, cc_ref, co_ref), (wa_ref, wb_ref, wc_ref, wo_ref), KW_WIDE)

    @pl.when(main)
    def _():
        is_sample = pl.program_id(0) - KW_WIDE >= ROWS_P // TM_MERGE
        c = jnp.where(is_sample, cs_ref[...], cp_ref[...])
        m = g0_ref[...].astype(F32) * _dot(a_ref[...], wa_ref[...])
        m = m + g1_ref[...].astype(F32) * _dot(b_ref[...], wb_ref[...])
        m = m + g2_ref[...].astype(F32) * _dot(c, wc_ref[...])
        h1 = h_ref[...] + _dot(m.astype(BF16), wo_ref[...])
        h1_ref[...] = h1
        hn_ref[...] = _rms(h1, gf_ref[...]).astype(BF16)


def _merge_call(a, b, c_p, c_s, gates, h, w_a, w_b, w_c, w_o, g_ffn, l):
    tm, kw = TM_MERGE, KW_WIDE
    np_ = ROWS_P // tm
    rows = functools.partial(_rows, tm=tm, lag=kw)
    return _call(
        _merge_body, (kw + ROWS // tm,),
        [rows(CONV_DIM), rows(GMLP_DIM),
         pl.BlockSpec((tm, ATTN_DIM), lambda s: (jnp.clip(s - kw, 0, np_ - 1), 0)),
         pl.BlockSpec((tm, ATTN_DIM), lambda s: (jnp.maximum(s - kw - np_, 0), 0)),
         rows(D_MODEL, cb=0), rows(D_MODEL, cb=1), rows(D_MODEL, cb=2), rows(D_MODEL),
         _wchunk(l, CONV_DIM, D_MODEL, kw), _wchunk(l, GMLP_DIM, D_MODEL, kw),
         _wchunk(l, ATTN_DIM, D_MODEL, kw), _wchunk(l, D_MODEL, D_MODEL, kw), _vec(l, D_MODEL)],
        [rows(D_MODEL), rows(D_MODEL)],
        [jax.ShapeDtypeStruct((ROWS, D_MODEL), F32), jax.ShapeDtypeStruct((ROWS, D_MODEL), BF16)],
        scratch=[_wcopy(CONV_DIM, D_MODEL), _wcopy(GMLP_DIM, D_MODEL), _wcopy(ATTN_DIM, D_MODEL),
                 _wcopy(D_MODEL, D_MODEL)],
        name="merge_out", vmem_mb=58)(a, b, c_p, c_s, gates, gates, gates, h, w_a, w_b, w_c, w_o, g_ffn)


def _ffn_body(hn_ref, w1_ref, w2_ref, h1_ref, o_ref):
    def delta():
        a = jnp.square(jnp.maximum(_dot(hn_ref[...], w1_ref[...]), 0.0)).astype(BF16)
        return _dot(a, w2_ref[...])

    @pl.when(pl.program_id(1) == 0)
    def _():
        o_ref[...] = h1_ref[...] + delta()

    @pl.when(pl.program_id(1) != 0)
    def _():
        o_ref[...] += delta()


def _ffn_call(hn, h1, w1, w2, l):
    return _call(
        _ffn_body, (NT, D_FF // TF),
        [pl.BlockSpec((TM, D_MODEL), lambda i, f: (i, 0)),
         pl.BlockSpec((None, D_MODEL, TF), lambda i, f: (l, 0, f)),
         pl.BlockSpec((None, TF, D_MODEL), lambda i, f: (l, f, 0)),
         pl.BlockSpec((TM, D_MODEL), lambda i, f: (i, 0))],
        pl.BlockSpec((TM, D_MODEL), lambda i, f: (i, 0)),
        jax.ShapeDtypeStruct((ROWS, D_MODEL), F32),
        name="ffn", vmem_mb=62)(hn, w1, w2, h1)


def _ple_update(tile, h2_ref, pp_ref, ps_ref, wg_ref, wp_ref, gp_ref):
    h2 = h2_ref[...]
    p = jnp.where(tile == NT_P, ps_ref[...], pp_ref[...])
    gate = jax.nn.sigmoid(_dot(_rms(h2, gp_ref[...]).astype(BF16), wg_ref[...]))
    return h2 + gate * _dot(p.astype(BF16), wp_ref[...])


def _ple_body(h2_ref, pp_ref, ps_ref, cg_ref, cp_ref, gp_ref, gn_ref, h3_ref, xn_ref, wg_ref, wp_ref):
    @pl.when(_stage_weights((cg_ref, cp_ref), (wg_ref, wp_ref), KW))
    def _():
        h3 = _ple_update(pl.program_id(0) - KW, h2_ref, pp_ref, ps_ref, wg_ref, wp_ref, gp_ref)
        h3_ref[...] = h3
        xn_ref[...] = _rms(h3, gn_ref[...]).astype(BF16)


def _ple_last_body(h2_ref, pp_ref, ps_ref, cg_ref, cp_ref, gp_ref, gn_ref, yp_ref, ys_ref, wg_ref, wp_ref):
    main = _stage_weights((cg_ref, cp_ref), (wg_ref, wp_ref), KW)
    tile = pl.program_id(0) - KW

    def final():
        return _rms(_ple_update(tile, h2_ref, pp_ref, ps_ref, wg_ref, wp_ref, gp_ref), gn_ref[...])

    @pl.when(main & (tile < NT_P))
    def _():
        yp_ref[...] = final()

    @pl.when(tile >= NT_P)
    def _():
        ys_ref[...] = final()


def _ple_call(h2, p_p, p_s, w_g, w_p, g_ple, g_next, l, last):
    assert ROWS_S == TM
    ins = [_rows(D_MODEL, lag=KW),
           pl.BlockSpec((None, TM, PLE_DIM), lambda s: (l, jnp.clip(s - KW, 0, NT_P - 1), 0)),
           pl.BlockSpec((None, TM, PLE_DIM), lambda s: (l, 0, 0)),
           _wchunk(l, D_MODEL, D_MODEL, KW), _wchunk(l, PLE_DIM, D_MODEL, KW),
           _vec(l, D_MODEL), _vec(l + 1, D_MODEL)]
    scratch = [_wcopy(D_MODEL, D_MODEL), _wcopy(PLE_DIM, D_MODEL)]
    args = (h2, p_p, p_s, w_g, w_p, g_ple, g_next)
    if not last:
        return _call(
            _ple_body, (KW + NT,), ins, [_rows(D_MODEL, lag=KW)] * 2,
            [jax.ShapeDtypeStruct((ROWS, D_MODEL), F32), jax.ShapeDtypeStruct((ROWS, D_MODEL), BF16)],
            scratch=scratch, name="ple", vmem_mb=56)(*args)
    return _call(
        _ple_last_body, (KW + NT,), ins,
        [pl.BlockSpec((TM, D_MODEL), lambda s: (jnp.clip(s - KW, 0, NT_P - 1), 0)),
         pl.BlockSpec((TM, D_MODEL), lambda s: (0, 0))],
        [jax.ShapeDtypeStruct((ROWS_P, D_MODEL), F32), jax.ShapeDtypeStruct((ROWS_S, D_MODEL), F32)],
        scratch=scratch, name="ple_final", vmem_mb=56)(*args)


def _conv_rows(ext_ref, y_ref, cw_ref, ext_base, y_base, rc):
    for lb in range(CONV_DIM // CONV_LB):
        ls = slice(lb * CONV_LB, (lb + 1) * CONV_LB)
        acc = None
        for r in range(SUBLANES):
            rows = rc + (SUBLANES if r else 0)
            u = None
            for j in range(CONV_WIDTH):
                if (2 + j) % SUBLANES == r:
                    start = ext_base + (2 + j - r)
                    if not isinstance(start, int):
                        start = pl.multiple_of(start, SUBLANES)
                    term = cw_ref[j:j + 1, ls] * ext_ref[pl.ds(start, rows), ls]
                    u = term if u is None else u + term
            part = u[r:r + rc, :]
            acc = part if acc is None else acc + part
        y_ref[pl.ds(y_base, rc), ls] = acc


def _conv_body(a_ref, cache_ref, cw_ref, cb_ref, g_ref, b_ref, o_ref, ext_ref, y_ref):
    i = pl.program_id(0)
    hist = 32

    @pl.when(i < NT_P)
    def _():
        @pl.when(i % TILES_PER_SEQ == 0)
        def _():
            ext_ref[0:hist, :] = jnp.zeros((hist, CONV_DIM), F32)

        @pl.when(i % TILES_PER_SEQ != 0)
        def _():
            ext_ref[0:hist, :] = ext_ref[TM:TM + hist, :]

        ext_ref[hist:hist + TM, :] = a_ref[...]

        def chunk(c, carry):
            r0 = pl.multiple_of(c * CONV_RC, CONV_RC)
            _conv_rows(ext_ref, y_ref, cw_ref, r0, r0, CONV_RC)
            return carry

        lax.fori_loop(0, TM // CONV_RC, chunk, 0)

    @pl.when(i == NT_P)
    def _():
        def seq(s, carry):
            r0 = pl.multiple_of(s * DEC_SEQ, DEC_SEQ)
            ext_ref[hist - CONV_HIST:hist, :] = cache_ref[s]
            ext_ref[hist:hist + DEC_SEQ, :] = a_ref[pl.ds(r0, DEC_SEQ), :]
            _conv_rows(ext_ref, y_ref, cw_ref, 0, r0, DEC_SEQ)
            return carry

        lax.fori_loop(0, DEC_BATCH, seq, 0)

    y = _ln(y_ref[...] + cb_ref[...], g_ref[...], b_ref[...])
    o_ref[...] = (y * jax.nn.sigmoid(y)).astype(BF16)


def _conv_call(a, cache_conv, conv_w, conv_b, ln_g, ln_b, l):
    return _call(
        _conv_body, (NT,),
        [_rows(CONV_DIM),
         _resident((None, DEC_BATCH, CONV_HIST, CONV_DIM), (l, 0, 0, 0)),
         _resident((None, CONV_WIDTH, CONV_DIM), (l, 0, 0)),
         _vec(l, CONV_DIM), _vec(l, CONV_DIM), _vec(l, CONV_DIM)],
        _rows(CONV_DIM), jax.ShapeDtypeStruct((ROWS, CONV_DIM), BF16),
        scratch=[pltpu.VMEM((TM + 32, CONV_DIM), F32), pltpu.VMEM((TM, CONV_DIM), F32)],
        name="conv_mixer", vmem_mb=32)(a, cache_conv, conv_w, conv_b, ln_g, ln_b)


def _toeplitz_rows(tab_ref, h, rows):
    t = jnp.broadcast_to(tab_ref[h:h + 1, :], (rows, TOEP))
    return pltpu.roll(t, TOEP - QB + 1, 1, stride=1, stride_axis=0)


def _attn_p_body(q_ref, *refs):
    nkb = QPS + 2
    k_refs, v_refs = refs[:nkb], refs[nkb:2 * nkb]
    tab_ref, o_ref, bias_ref = refs[2 * nkb:]
    m = pl.program_id(1)

    @pl.when((pl.program_id(0) == 0) & (m == 0))
    def _():
        r = lax.broadcasted_iota(jnp.int32, (QB, KB), 0) // CHUNK
        j = lax.broadcasted_iota(jnp.int32, (QB, KB), 1) // CHUNK
        band = (j >= r) & (j <= r + LEFT_CHUNKS)
        for h in range(N_HEADS):
            bias = jnp.where(band, _toeplitz_rows(tab_ref, h, QB)[:, :KB], NEG)
            for s in range(3):
                bias_ref[h, s] = bias[:, s * QB:(s + 1) * QB]
            bias_ref[h, 3] = jnp.full((QB, QB), NEG, F32)

    for j in range(QPS):
        rs = slice(j * QB, (j + 1) * QB)
        slot = [jnp.where(QPS * m + j + s - 2 >= 0, s, 3) for s in range(3)]
        for h in range(N_HEADS):
            hs = slice(h * HEAD_DIM, (h + 1) * HEAD_DIM)
            qh = q_ref[rs, hs]
            sc = [_dot_nt(qh, k_refs[j + s][:, hs]) + bias_ref[h, slot[s]] for s in range(3)]
            mx = jnp.max(functools.reduce(jnp.maximum, sc), axis=-1, keepdims=True)
            e = [jnp.exp2(x - mx) for x in sc]
            den = jnp.sum(functools.reduce(jnp.add, e), axis=-1, keepdims=True)
            o = functools.reduce(jnp.add, [_dot(e[s].astype(BF16), v_refs[j + s][:, hs]) for s in range(3)])
            o_ref[rs, hs] = (o * (1.0 / den)).astype(BF16)


def _attn_p_call(q, k, v, toep, l):
    nq = SEQ // QB
    steps = nq // QPS
    nkb = QPS + 2

    def kv_spec(t):
        return pl.BlockSpec((QB, ATTN_DIM), lambda b, m: (b * nq + jnp.maximum(QPS * m - 2 + t, 0), 0))

    q_spec = pl.BlockSpec((QPS * QB, ATTN_DIM), lambda b, m: (b * steps + m, 0))
    return _call(
        _attn_p_body, (BATCH, steps),
        [q_spec] + [kv_spec(t) for t in range(nkb)] * 2 + [_resident((None, N_HEADS, TOEP), (l, 0, 0))],
        q_spec, jax.ShapeDtypeStruct((ROWS_P, ATTN_DIM), BF16),
        scratch=[pltpu.VMEM((N_HEADS, 4, QB, QB), F32)],
        name="attn_prompt", vmem_mb=48)(q, *([k] * nkb), *([v] * nkb), toep)


def _attn_s_body(q_ref, kn_ref, vn_ref, kc_ref, vc_ref, tab_ref, o_ref, bias_ref):
    w = ATTN_REACH

    @pl.when(pl.program_id(0) == 0)
    def _():
        for h in range(N_HEADS):
            bias_ref[h] = _toeplitz_rows(tab_ref, h, DEC_SEQ)[:, :w + DEC_SEQ]

    for b in range(SEQS_PER_STEP):
        rs = slice(b * DEC_SEQ, (b + 1) * DEC_SEQ)
        for h in range(N_HEADS):
            hs = slice(h * HEAD_DIM, (h + 1) * HEAD_DIM)
            rows_h = pl.ds(h, w, stride=N_HEADS)
            qh = q_ref[rs, hs]
            s_c = _dot_nt(qh, kc_ref[b, rows_h, :].astype(BF16)) + bias_ref[h, :, :w]
            s_n = _dot_nt(qh, kn_ref[rs, hs]) + bias_ref[h, :, w:]
            mx = jnp.maximum(jnp.max(s_c, axis=-1, keepdims=True), jnp.max(s_n, axis=-1, keepdims=True))
            e_c = jnp.exp2(s_c - mx)
            e_n = jnp.exp2(s_n - mx)
            den = jnp.sum(e_c, axis=-1, keepdims=True) + jnp.sum(e_n, axis=-1, keepdims=True)
            o = (_dot(e_c.astype(BF16), vc_ref[b, rows_h, :].astype(BF16))
                 + _dot(e_n.astype(BF16), vn_ref[rs, hs]))
            o_ref[rs, hs] = (o * (1.0 / den)).astype(BF16)


def _attn_s_call(q, k, v, cache_k, cache_v, toep, l):
    nb, rows = SEQS_PER_STEP, SEQS_PER_STEP * DEC_SEQ
    first = ROWS_P // rows
    new = pl.BlockSpec((rows, ATTN_DIM), lambda i: (first + i, 0))
    cache = pl.BlockSpec((None, nb, ATTN_REACH * N_HEADS, HEAD_DIM), lambda i: (l, i, 0, 0))
    return _call(
        _attn_s_body, (DEC_BATCH // nb,),
        [new, new, new, cache, cache, _resident((None, N_HEADS, TOEP), (l, 0, 0))],
        pl.BlockSpec((rows, ATTN_DIM), lambda i: (i, 0)),
        jax.ShapeDtypeStruct((ROWS_S, ATTN_DIM), BF16),
        scratch=[pltpu.VMEM((N_HEADS, DEC_SEQ, ATTN_REACH + DEC_SEQ), F32)],
        name="attn_sample", vmem_mb=40)(q, k, v, cache_k, cache_v, toep)


def _toeplitz_table(rel_tab):
    left = 3 * QB - 1 - REL_CLIP
    right = TOEP - left - (2 * REL_CLIP + 1)
    pad = [(0, 0)] * (rel_tab.ndim - 1) + [(left, right)]
    return jnp.pad(rel_tab[..., ::-1] * LOG2E, pad, mode="edge").astype(F32)


def kernel(x_prompt, x_sample, p_prompt, p_sample, cache_conv, cache_k, cache_v, norm_mix, w_in, conv_w, conv_b, conv_ln_g, conv_ln_b, w_a_out, gmlp_ln_g, gmlp_ln_b, gmlp_ws, gmlp_bs, w_b_out, attn_rel_bias, w_c_out, w_o, norm_ffn, w_ff1, w_ff2, norm_ple, w_ple_gate, w_ple_proj, norm_final):
    p_p = p_prompt.reshape(DEPTH, ROWS_P, PLE_DIM)
    p_s = p_sample.reshape(DEPTH, ROWS_S, PLE_DIM)
    cache_k = cache_k.reshape(DEPTH, DEC_BATCH, ATTN_REACH * N_HEADS, HEAD_DIM)
    cache_v = cache_v.reshape(DEPTH, DEC_BATCH, ATTN_REACH * N_HEADS, HEAD_DIM)
    toep = _toeplitz_table(attn_rel_bias)

    vec = lambda x: x.reshape(x.shape[0], 1, x.shape[-1])
    g_mix = vec(jnp.concatenate([norm_mix, norm_final[None]], axis=0))
    conv_b3, conv_g3, conv_lb3 = vec(conv_b), vec(conv_ln_g), vec(conv_ln_b)
    gm_g3, gm_b3 = vec(gmlp_ln_g), vec(gmlp_ln_b)
    g_ffn3, g_ple3 = vec(norm_ffn), vec(norm_ple)

    w_ff1_b, w_ff2_b = w_ff1.astype(BF16), w_ff2.astype(BF16)

    outs = [[] for _ in range(7)]
    h, xn = _first_call(x_prompt.reshape(ROWS_P, D_MODEL), x_sample.reshape(ROWS_S, D_MODEL), g_mix)
    for l in range(DEPTH):
        q, k, v, k_rows, v_rows = _qkv_call(xn, w_in, l)
        gates = _gate_call(xn, w_in, l)

        ws = gmlp_ws[l]
        corner = jnp.tile(ws[:, :DEC_SEQ, :DEC_SEQ], (1, GMLP_CHUNK // DEC_SEQ, GMLP_CHUNK // DEC_SEQ))
        bias_p = jnp.repeat(gmlp_bs[l].T, GMLP_GC, axis=1)
        bias_s = jnp.tile(bias_p[:DEC_SEQ], (GMLP_CHUNK // DEC_SEQ, 1))
        a, yb_in, v_s = _glu_gmlp_call(xn, w_in, gm_g3, gm_b3, jnp.stack([ws, corner]),
                                       jnp.stack([bias_p, bias_s]), l)
        ya_in = _conv_call(a, cache_conv, conv_w, conv_b3, conv_g3, conv_lb3, l)

        o_p = _attn_p_call(q, k, v, toep, l)
        o_s = _attn_s_call(q, k, v, cache_k, cache_v, toep, l)

        h1, hn = _merge_call(ya_in, yb_in, o_p, o_s, gates, h, w_a_out, w_b_out, w_c_out, w_o, g_ffn3, l)
        h2 = _ffn_call(hn, h1, w_ff1_b, w_ff2_b, l)
        last = l == DEPTH - 1
        h, xn = _ple_call(h2, p_p, p_s, w_ple_gate, w_ple_proj, g_ple3, g_mix, l, last)

        outs[0].append(jnp.stack([a[(b + 1) * SEQ - CONV_HIST:(b + 1) * SEQ] for b in range(BATCH)]))
        outs[1].append(a[ROWS_P:].reshape(DEC_BATCH, DEC_SEQ, CONV_DIM)[:, DEC_SEQ - CONV_HIST:])
        for j, rows in enumerate((k_rows, v_rows)):
            rows = rows.reshape(KV_SLOTS, TM, N_HEADS, HEAD_DIM)
            outs[2 + j].append(rows[:BATCH])
            outs[4 + j].append(rows[BATCH].reshape(DEC_BATCH, DEC_SEQ, N_HEADS, HEAD_DIM))
        outs[6].append(v_s.reshape(DEC_BATCH, DEC_SEQ, GMLP_DIM))

    y_prompt = h.reshape(BATCH, SEQ, D_MODEL)
    y_sample = xn.reshape(DEC_BATCH, DEC_SEQ, D_MODEL)
    return (y_prompt, y_sample) + tuple(jnp.stack(x) for x in outs)
```

```python
import functools

import jax
import jax.numpy as jnp
from jax import lax
from jax.experimental import pallas as pl
from jax.experimental.pallas import tpu as pltpu

F32 = jnp.float32
BF16 = jnp.bfloat16

D_MODEL = 2048
BATCH = 2
SEQ = 4096
DEPTH = 4
DEC_BATCH = 16
DEC_SEQ = 32
CHUNK = 64
CONV_DIM = 1024
CONV_WIDTH = 31
CONV_HIST = CONV_WIDTH - 1
GMLP_DIM = 1024
GMLP_GROUPS = 8
GMLP_GC = GMLP_DIM // GMLP_GROUPS
GMLP_CHUNK = 128
N_HEADS = 8
HEAD_DIM = 128
ATTN_DIM = N_HEADS * HEAD_DIM
LEFT_CHUNKS = 8
ATTN_REACH = LEFT_CHUNKS * CHUNK
REL_CLIP = 128
D_FF = 4 * D_MODEL
PLE_DIM = 256
EPS = 1e-6
SCALE = HEAD_DIM ** -0.5
LOG2E = 1.4426950408889634
NEG = -1e30

ROWS_P = BATCH * SEQ
ROWS_S = DEC_BATCH * DEC_SEQ
ROWS = ROWS_P + ROWS_S
TM = 512
NT = ROWS // TM
NT_P = ROWS_P // TM
TILES_PER_SEQ = SEQ // TM
TM_MERGE = 256
TF = 2048
COLB = 1024
QB = 256
KB = 3 * QB
QPS = 2
CONV_RC = 64
CONV_LB = 128
SUBLANES = 8
TOEP = 1024
KW = 8
KW_WIDE = 8
SEQS_PER_STEP = 2
MIB = 1 << 20


def _rms(x, g):
    return x * lax.rsqrt(jnp.mean(x * x, axis=-1, keepdims=True) + EPS) * g


def _ln(x, g, b):
    mu = jnp.mean(x, axis=-1, keepdims=True)
    xc = x - mu
    var = jnp.mean(xc * xc, axis=-1, keepdims=True)
    return xc * lax.rsqrt(var + EPS) * g + b


def _dot(a, b):
    return jnp.dot(a, b, preferred_element_type=F32)


def _dot_nt(a, b):
    return lax.dot_general(a, b, (((1,), (1,)), ((), ())), preferred_element_type=F32)


def _call(body, grid, in_specs, out_specs, out_shape, *, name, vmem_mb, scratch=()):
    return pl.pallas_call(
        body,
        grid=grid,
        in_specs=in_specs,
        out_specs=out_specs,
        out_shape=out_shape,
        scratch_shapes=list(scratch),
        compiler_params=pltpu.CompilerParams(
            dimension_semantics=("arbitrary",) * len(grid), vmem_limit_bytes=vmem_mb * MIB),
        name=name,
    )


def _rows(cols, tm=TM, cb=0, lag=0):
    return pl.BlockSpec((tm, cols), lambda s, *_: (jnp.maximum(s - lag, 0), cb))


def _resident(block, index):
    return pl.BlockSpec(block, lambda *_: index, pipeline_mode=pl.Buffered(1))


def _vec(l, n):
    return _resident((None, 1, n), (l, 0, 0))


def _wchunk(l, k, cols, kw, cb=0):
    return pl.BlockSpec((None, k // kw, cols), lambda s, *_: (l, jnp.minimum(s, kw - 1), cb))


def _wcopy(k, cols):
    return pltpu.VMEM((k, cols), BF16)


def _stage_weights(chunk_refs, w_refs, kw):
    s = pl.program_id(0)

    @pl.when(s < kw)
    def _():
        for c_ref, w_ref in zip(chunk_refs, w_refs):
            rows = c_ref.shape[0]
            w_ref[pl.ds(pl.multiple_of(s * rows, rows), rows), :] = c_ref[...].astype(BF16)

    return s >= kw


def _group_specs(cols):
    assert ROWS_S == TM
    return [pl.BlockSpec((TM, cols), lambda i: (jnp.minimum(i, NT_P - 1), 0)),
            pl.BlockSpec((TM, cols), lambda i: (0, 0))]


def _first_body(xp_ref, xs_ref, g_ref, h_ref, xn_ref):
    x = jnp.where(pl.program_id(0) == NT_P, xs_ref[...], xp_ref[...])
    h_ref[...] = x
    xn_ref[...] = _rms(x, g_ref[...]).astype(BF16)


def _first_call(xp, xs, g3):
    return _call(
        _first_body, (NT,), _group_specs(D_MODEL) + [_vec(0, D_MODEL)], [_rows(D_MODEL)] * 2,
        [jax.ShapeDtypeStruct((ROWS, D_MODEL), F32), jax.ShapeDtypeStruct((ROWS, D_MODEL), BF16)],
        name="rms_first", vmem_mb=40)(xp, xs, g3)


def _glu_gmlp_body(x_ref, ca_ref, cg_ref, cu_ref, cv_ref, g_ref, b_ref, ws_ref, bias_ref,
                   a_ref, yb_ref, vs_ref, wa_ref, wg_ref, wu_ref, wv_ref):
    main = _stage_weights((ca_ref, cg_ref, cu_ref, cv_ref), (wa_ref, wg_ref, wu_ref, wv_ref), KW_WIDE)

    @pl.when(main)
    def _():
        x = x_ref[...]
        v = _ln(jax.nn.gelu(_dot(x, wv_ref[...])), g_ref[...], b_ref[...])
        vs_ref[...] = v
        u = jax.nn.gelu(_dot(x, wu_ref[...]))
        a_ref[...] = _dot(x, wa_ref[...]) * jax.nn.sigmoid(_dot(x, wg_ref[...]))
        shift = jnp.where(pl.program_id(0) - KW_WIDE == NT_P, 5, 7)
        r = lax.broadcasted_iota(jnp.int32, (GMLP_CHUNK, GMLP_CHUNK), 0)
        c = lax.broadcasted_iota(jnp.int32, (GMLP_CHUNK, GMLP_CHUNK), 1)
        mask = (r >= c) & (lax.shift_right_logical(r, shift) == lax.shift_right_logical(c, shift))
        for g in range(GMLP_GROUPS):
            cs = slice(g * GMLP_GC, (g + 1) * GMLP_GC)
            wg = jnp.where(mask, ws_ref[g], 0.0).astype(BF16)
            bias = bias_ref[:, cs]
            for k in range(TM // GMLP_CHUNK):
                rs = slice(k * GMLP_CHUNK, (k + 1) * GMLP_CHUNK)
                mixed = _dot(wg, v[rs, cs].astype(BF16)) + bias
                yb_ref[rs, cs] = (u[rs, cs] * mixed).astype(BF16)


def _glu_gmlp_call(xn, w_in, ln_g, ln_b, wmix, bmix, l):
    assert ROWS_S == TM
    kw = KW_WIDE
    pick = lambda s: (jnp.where(s - kw == NT_P, 1, 0), 0, 0, 0)
    return _call(
        _glu_gmlp_body, (kw + NT,),
        [_rows(D_MODEL, lag=kw)] + [_wchunk(l, D_MODEL, COLB, kw, j) for j in range(4)]
        + [_vec(l, GMLP_DIM), _vec(l, GMLP_DIM),
           pl.BlockSpec((None, GMLP_GROUPS, GMLP_CHUNK, GMLP_CHUNK), pick),
           pl.BlockSpec((None, GMLP_CHUNK, GMLP_DIM), lambda s: pick(s)[:3])],
        [_rows(CONV_DIM, lag=kw), _rows(GMLP_DIM, lag=kw), pl.BlockSpec((TM, GMLP_DIM), lambda s: (0, 0))],
        [jax.ShapeDtypeStruct((ROWS, CONV_DIM), F32), jax.ShapeDtypeStruct((ROWS, GMLP_DIM), BF16),
         jax.ShapeDtypeStruct((ROWS_S, GMLP_DIM), F32)],
        scratch=[_wcopy(D_MODEL, COLB)] * 4,
        name="inproj_glu_gmlp", vmem_mb=58)(xn, w_in, w_in, w_in, w_in, ln_g, ln_b, wmix, bmix)


KV_SLOTS = BATCH + 1


def _qkv_body(x_ref, cq_ref, ck_ref, cv_ref, q_ref, k_ref, v_ref, kr_ref, vr_ref, wq_ref, wk_ref, wv_ref):
    @pl.when(_stage_weights((cq_ref, ck_ref, cv_ref), (wq_ref, wk_ref, wv_ref), KW))
    def _():
        x = x_ref[...]
        q_ref[...] = (_dot(x, wq_ref[...]) * (SCALE * LOG2E)).astype(BF16)
        for w_ref, o_ref, rows_ref in ((wk_ref, k_ref, kr_ref), (wv_ref, v_ref, vr_ref)):
            y = _dot(x, w_ref[...])
            o_ref[...] = y.astype(BF16)
            for h in range(N_HEADS):
                rows_ref[pl.ds(h, TM, stride=N_HEADS), :] = y[:, h * HEAD_DIM:(h + 1) * HEAD_DIM]


def _qkv_call(xn, w_in, l):
    act = jax.ShapeDtypeStruct((ROWS, ATTN_DIM), BF16)
    rows = jax.ShapeDtypeStruct((KV_SLOTS * TM * N_HEADS, HEAD_DIM), F32)
    kv_slot = lambda s: (jnp.maximum(s - KW, 0) // TILES_PER_SEQ, 0)
    return _call(
        _qkv_body, (KW + NT,),
        [_rows(D_MODEL, lag=KW)] + [_wchunk(l, D_MODEL, COLB, KW, 4 + j) for j in range(3)],
        [_rows(ATTN_DIM, lag=KW)] * 3 + [pl.BlockSpec((TM * N_HEADS, HEAD_DIM), kv_slot)] * 2,
        [act, act, act, rows, rows],
        scratch=[_wcopy(D_MODEL, COLB)] * 3,
        name="inproj_qkv", vmem_mb=52)(xn, w_in, w_in, w_in)


N_GATE_BLOCKS = 3 * D_MODEL // COLB


def _gate_body(x_ref, *refs):
    n = N_GATE_BLOCKS
    c_refs, o_ref, w_refs = refs[:n], refs[n], refs[n + 1:]

    @pl.when(_stage_weights(c_refs, w_refs, KW_WIDE))
    def _():
        x = x_ref[...]
        for j, w_ref in enumerate(w_refs):
            o_ref[:, j * COLB:(j + 1) * COLB] = jax.nn.sigmoid(_dot(x, w_ref[...])).astype(BF16)


def _gate_call(xn, w_in, l):
    n = N_GATE_BLOCKS
    return _call(
        _gate_body, (KW_WIDE + NT,),
        [_rows(D_MODEL, lag=KW_WIDE)] + [_wchunk(l, D_MODEL, COLB, KW_WIDE, 7 + j) for j in range(n)],
        _rows(3 * D_MODEL, lag=KW_WIDE), jax.ShapeDtypeStruct((ROWS, 3 * D_MODEL), BF16),
        scratch=[_wcopy(D_MODEL, COLB)] * n,
        name="inproj_gates", vmem_mb=56)(xn, *([w_in] * n))


def _merge_body(a_ref, b_ref, cp_ref, cs_ref, g0_ref, g1_ref, g2_ref, h_ref,
                ca_ref, cb_ref, cc_ref, co_ref, gf_ref, h1_ref, hn_ref,
                wa_ref, wb_ref, wc_ref, wo_ref):
    main = _stage_weights((ca_ref, cb_ref, cc_ref, co_ref), (wa_ref, wb_ref, wc_ref, wo_ref), KW_WIDE)

    @pl.when(main)
    def _():
        is_sample = pl.program_id(0) - KW_WIDE >= ROWS_P // TM_MERGE
        c = jnp.where(is_sample, cs_ref[...], cp_ref[...])
        m = g0_ref[...].astype(F32) * _dot(a_ref[...], wa_ref[...])
        m = m + g1_ref[...].astype(F32) * _dot(b_ref[...], wb_ref[...])
        m = m + g2_ref[...].astype(F32) * _dot(c, wc_ref[...])
        h1 = h_ref[...] + _dot(m.astype(BF16), wo_ref[...])
        h1_ref[...] = h1
        hn_ref[...] = _rms(h1, gf_ref[...]).astype(BF16)


def _merge_call(a, b, c_p, c_s, gates, h, w_a, w_b, w_c, w_o, g_ffn, l):
    tm, kw = TM_MERGE, KW_WIDE
    np_ = ROWS_P // tm
    rows = functools.partial(_rows, tm=tm, lag=kw)
    return _call(
        _merge_body, (kw + ROWS // tm,),
        [rows(CONV_DIM), rows(GMLP_DIM),
         pl.BlockSpec((tm, ATTN_DIM), lambda s: (jnp.clip(s - kw, 0, np_ - 1), 0)),
         pl.BlockSpec((tm, ATTN_DIM), lambda s: (jnp.maximum(s - kw - np_, 0), 0)),
         rows(D_MODEL, cb=0), rows(D_MODEL, cb=1), rows(D_MODEL, cb=2), rows(D_MODEL),
         _wchunk(l, CONV_DIM, D_MODEL, kw), _wchunk(l, GMLP_DIM, D_MODEL, kw),
         _wchunk(l, ATTN_DIM, D_MODEL, kw), _wchunk(l, D_MODEL, D_MODEL, kw), _vec(l, D_MODEL)],
        [rows(D_MODEL), rows(D_MODEL)],
        [jax.ShapeDtypeStruct((ROWS, D_MODEL), F32), jax.ShapeDtypeStruct((ROWS, D_MODEL), BF16)],
        scratch=[_wcopy(CONV_DIM, D_MODEL), _wcopy(GMLP_DIM, D_MODEL), _wcopy(ATTN_DIM, D_MODEL),
                 _wcopy(D_MODEL, D_MODEL)],
        name="merge_out", vmem_mb=58)(a, b, c_p, c_s, gates, gates, gates, h, w_a, w_b, w_c, w_o, g_ffn)


def _ffn_body(hn_ref, w1_ref, w2_ref, h1_ref, o_ref):
    def delta():
        a = jnp.square(jnp.maximum(_dot(hn_ref[...], w1_ref[...]), 0.0)).astype(BF16)
        return _dot(a, w2_ref[...])

    @pl.when(pl.program_id(1) == 0)
    def _():
        o_ref[...] = h1_ref[...] + delta()

    @pl.when(pl.program_id(1) != 0)
    def _():
        o_ref[...] += delta()


def _ffn_call(hn, h1, w1, w2):
    return _call(
        _ffn_body, (NT, D_FF // TF),
        [pl.BlockSpec((TM, D_MODEL), lambda i, f: (i, 0)),
         pl.BlockSpec((D_MODEL, TF), lambda i, f: (0, f)),
         pl.BlockSpec((TF, D_MODEL), lambda i, f: (f, 0)),
         pl.BlockSpec((TM, D_MODEL), lambda i, f: (i, 0))],
        pl.BlockSpec((TM, D_MODEL), lambda i, f: (i, 0)),
        jax.ShapeDtypeStruct((ROWS, D_MODEL), F32),
        name="ffn", vmem_mb=62)(hn, w1, w2, h1)


def _ple_update(tile, h2_ref, pp_ref, ps_ref, wg_ref, wp_ref, gp_ref):
    h2 = h2_ref[...]
    p = jnp.where(tile == NT_P, ps_ref[...], pp_ref[...])
    gate = jax.nn.sigmoid(_dot(_rms(h2, gp_ref[...]).astype(BF16), wg_ref[...]))
    return h2 + gate * _dot(p.astype(BF16), wp_ref[...])


def _ple_body(h2_ref, pp_ref, ps_ref, cg_ref, cp_ref, gp_ref, gn_ref, h3_ref, xn_ref, wg_ref, wp_ref):
    @pl.when(_stage_weights((cg_ref, cp_ref), (wg_ref, wp_ref), KW))
    def _():
        h3 = _ple_update(pl.program_id(0) - KW, h2_ref, pp_ref, ps_ref, wg_ref, wp_ref, gp_ref)
        h3_ref[...] = h3
        xn_ref[...] = _rms(h3, gn_ref[...]).astype(BF16)


def _ple_last_body(h2_ref, pp_ref, ps_ref, cg_ref, cp_ref, gp_ref, gn_ref, yp_ref, ys_ref, wg_ref, wp_ref):
    main = _stage_weights((cg_ref, cp_ref), (wg_ref, wp_ref), KW)
    tile = pl.program_id(0) - KW

    def final():
        return _rms(_ple_update(tile, h2_ref, pp_ref, ps_ref, wg_ref, wp_ref, gp_ref), gn_ref[...])

    @pl.when(main & (tile < NT_P))
    def _():
        yp_ref[...] = final()

    @pl.when(tile >= NT_P)
    def _():
        ys_ref[...] = final()


def _ple_call(h2, p_p, p_s, w_g, w_p, g_ple, g_next, l, last):
    assert ROWS_S == TM
    ins = [_rows(D_MODEL, lag=KW),
           pl.BlockSpec((None, TM, PLE_DIM), lambda s: (l, jnp.clip(s - KW, 0, NT_P - 1), 0)),
           pl.BlockSpec((None, TM, PLE_DIM), lambda s: (l, 0, 0)),
           _wchunk(l, D_MODEL, D_MODEL, KW), _wchunk(l, PLE_DIM, D_MODEL, KW),
           _vec(l, D_MODEL), _vec(l + 1, D_MODEL)]
    scratch = [_wcopy(D_MODEL, D_MODEL), _wcopy(PLE_DIM, D_MODEL)]
    args = (h2, p_p, p_s, w_g, w_p, g_ple, g_next)
    if not last:
        return _call(
            _ple_body, (KW + NT,), ins, [_rows(D_MODEL, lag=KW)] * 2,
            [jax.ShapeDtypeStruct((ROWS, D_MODEL), F32), jax.ShapeDtypeStruct((ROWS, D_MODEL), BF16)],
            scratch=scratch, name="ple", vmem_mb=56)(*args)
    return _call(
        _ple_last_body, (KW + NT,), ins,
        [pl.BlockSpec((TM, D_MODEL), lambda s: (jnp.clip(s - KW, 0, NT_P - 1), 0)),
         pl.BlockSpec((TM, D_MODEL), lambda s: (0, 0))],
        [jax.ShapeDtypeStruct((ROWS_P, D_MODEL), F32), jax.ShapeDtypeStruct((ROWS_S, D_MODEL), F32)],
        scratch=scratch, name="ple_final", vmem_mb=56)(*args)


def _conv_rows(ext_ref, y_ref, cw_ref, ext_base, y_base, rc):
    for lb in range(CONV_DIM // CONV_LB):
        ls = slice(lb * CONV_LB, (lb + 1) * CONV_LB)
        acc = None
        for r in range(SUBLANES):
            rows = rc + (SUBLANES if r else 0)
            u = None
            for j in range(CONV_WIDTH):
                if (2 + j) % SUBLANES == r:
                    start = ext_base + (2 + j - r)
                    if not isinstance(start, int):
                        start = pl.multiple_of(start, SUBLANES)
                    term = cw_ref[j:j + 1, ls] * ext_ref[pl.ds(start, rows), ls]
                    u = term if u is None else u + term
            part = u[r:r + rc, :]
            acc = part if acc is None else acc + part
        y_ref[pl.ds(y_base, rc), ls] = acc


def _conv_body(a_ref, cache_ref, cw_ref, cb_ref, g_ref, b_ref, f1_ref, f2_ref,
               o_ref, f1b_ref, f2b_ref, ext_ref, y_ref):
    i = pl.program_id(0)
    hist = 32

    f1b_ref[...] = f1_ref[...].astype(BF16)
    f2b_ref[...] = f2_ref[...].astype(BF16)

    @pl.when(i < NT_P)
    def _():
        @pl.when(i % TILES_PER_SEQ == 0)
        def _():
            ext_ref[0:hist, :] = jnp.zeros((hist, CONV_DIM), F32)

        @pl.when(i % TILES_PER_SEQ != 0)
        def _():
            ext_ref[0:hist, :] = ext_ref[TM:TM + hist, :]

        ext_ref[hist:hist + TM, :] = a_ref[...]

        def chunk(c, carry):
            r0 = pl.multiple_of(c * CONV_RC, CONV_RC)
            _conv_rows(ext_ref, y_ref, cw_ref, r0, r0, CONV_RC)
            return carry

        lax.fori_loop(0, TM // CONV_RC, chunk, 0)

    @pl.when(i == NT_P)
    def _():
        def seq(s, carry):
            r0 = pl.multiple_of(s * DEC_SEQ, DEC_SEQ)
            ext_ref[hist - CONV_HIST:hist, :] = cache_ref[s]
            ext_ref[hist:hist + DEC_SEQ, :] = a_ref[pl.ds(r0, DEC_SEQ), :]
            _conv_rows(ext_ref, y_ref, cw_ref, 0, r0, DEC_SEQ)
            return carry

        lax.fori_loop(0, DEC_BATCH, seq, 0)

    y = _ln(y_ref[...] + cb_ref[...], g_ref[...], b_ref[...])
    o_ref[...] = (y * jax.nn.sigmoid(y)).astype(BF16)


FF_SLABS = NT - 1


def _conv_call(a, cache_conv, conv_w, conv_b, ln_g, ln_b, w_ff1, w_ff2, l):
    r1, r2 = D_MODEL // FF_SLABS, D_FF // FF_SLABS
    slab = lambda i: jnp.minimum(i, FF_SLABS - 1)
    return _call(
        _conv_body, (NT,),
        [_rows(CONV_DIM),
         _resident((None, DEC_BATCH, CONV_HIST, CONV_DIM), (l, 0, 0, 0)),
         _resident((None, CONV_WIDTH, CONV_DIM), (l, 0, 0)),
         _vec(l, CONV_DIM), _vec(l, CONV_DIM), _vec(l, CONV_DIM),
         pl.BlockSpec((None, r1, D_FF), lambda i: (l, slab(i), 0)),
         pl.BlockSpec((None, r2, D_MODEL), lambda i: (l, slab(i), 0))],
        [_rows(CONV_DIM), pl.BlockSpec((r1, D_FF), lambda i: (slab(i), 0)),
         pl.BlockSpec((r2, D_MODEL), lambda i: (slab(i), 0))],
        [jax.ShapeDtypeStruct((ROWS, CONV_DIM), BF16), jax.ShapeDtypeStruct((D_MODEL, D_FF), BF16),
         jax.ShapeDtypeStruct((D_FF, D_MODEL), BF16)],
        scratch=[pltpu.VMEM((TM + 32, CONV_DIM), F32), pltpu.VMEM((TM, CONV_DIM), F32)],
        name="conv_mixer", vmem_mb=48)(a, cache_conv, conv_w, conv_b, ln_g, ln_b, w_ff1, w_ff2)


def _toeplitz_rows(tab_ref, h, rows):
    t = jnp.broadcast_to(tab_ref[h:h + 1, :], (rows, TOEP))
    return pltpu.roll(t, TOEP - QB + 1, 1, stride=1, stride_axis=0)


def _attn_p_body(q_ref, *refs):
    nkb = QPS + 2
    k_refs, v_refs = refs[:nkb], refs[nkb:2 * nkb]
    tab_ref, o_ref, bias_ref = refs[2 * nkb:]
    m = pl.program_id(1)

    @pl.when((pl.program_id(0) == 0) & (m == 0))
    def _():
        r = lax.broadcasted_iota(jnp.int32, (QB, KB), 0) // CHUNK
        j = lax.broadcasted_iota(jnp.int32, (QB, KB), 1) // CHUNK
        band = (j >= r) & (j <= r + LEFT_CHUNKS)
        for h in range(N_HEADS):
            bias = jnp.where(band, _toeplitz_rows(tab_ref, h, QB)[:, :KB], NEG)
            for s in range(3):
                bias_ref[h, s] = bias[:, s * QB:(s + 1) * QB]
            bias_ref[h, 3] = jnp.full((QB, QB), NEG, F32)

    for j in range(QPS):
        rs = slice(j * QB, (j + 1) * QB)
        slot = [jnp.where(QPS * m + j + s - 2 >= 0, s, 3) for s in range(3)]
        for h in range(N_HEADS):
            hs = slice(h * HEAD_DIM, (h + 1) * HEAD_DIM)
            qh = q_ref[rs, hs]
            sc = [_dot_nt(qh, k_refs[j + s][:, hs]) + bias_ref[h, slot[s]] for s in range(3)]
            mx = jnp.max(functools.reduce(jnp.maximum, sc), axis=-1, keepdims=True)
            e = [jnp.exp2(x - mx) for x in sc]
            den = jnp.sum(functools.reduce(jnp.add, e), axis=-1, keepdims=True)
            o = functools.reduce(jnp.add, [_dot(e[s].astype(BF16), v_refs[j + s][:, hs]) for s in range(3)])
            o_ref[rs, hs] = (o * (1.0 / den)).astype(BF16)


def _attn_p_call(q, k, v, toep, l):
    nq = SEQ // QB
    steps = nq // QPS
    nkb = QPS + 2

    def kv_spec(t):
        return pl.BlockSpec((QB, ATTN_DIM), lambda b, m: (b * nq + jnp.maximum(QPS * m - 2 + t, 0), 0))

    q_spec = pl.BlockSpec((QPS * QB, ATTN_DIM), lambda b, m: (b * steps + m, 0))
    return _call(
        _attn_p_body, (BATCH, steps),
        [q_spec] + [kv_spec(t) for t in range(nkb)] * 2 + [_resident((None, N_HEADS, TOEP), (l, 0, 0))],
        q_spec, jax.ShapeDtypeStruct((ROWS_P, ATTN_DIM), BF16),
        scratch=[pltpu.VMEM((N_HEADS, 4, QB, QB), F32)],
        name="attn_prompt", vmem_mb=48)(q, *([k] * nkb), *([v] * nkb), toep)


def _attn_s_body(q_ref, kn_ref, vn_ref, kc_ref, vc_ref, tab_ref, o_ref, bias_ref):
    w = ATTN_REACH

    @pl.when(pl.program_id(0) == 0)
    def _():
        for h in range(N_HEADS):
            bias_ref[h] = _toeplitz_rows(tab_ref, h, DEC_SEQ)[:, :w + DEC_SEQ]

    for b in range(SEQS_PER_STEP):
        rs = slice(b * DEC_SEQ, (b + 1) * DEC_SEQ)
        for h in range(N_HEADS):
            hs = slice(h * HEAD_DIM, (h + 1) * HEAD_DIM)
            rows_h = pl.ds(h, w, stride=N_HEADS)
            qh = q_ref[rs, hs]
            s_c = _dot_nt(qh, kc_ref[b, rows_h, :].astype(BF16)) + bias_ref[h, :, :w]
            s_n = _dot_nt(qh, kn_ref[rs, hs]) + bias_ref[h, :, w:]
            mx = jnp.maximum(jnp.max(s_c, axis=-1, keepdims=True), jnp.max(s_n, axis=-1, keepdims=True))
            e_c = jnp.exp2(s_c - mx)
            e_n = jnp.exp2(s_n - mx)
            den = jnp.sum(e_c, axis=-1, keepdims=True) + jnp.sum(e_n, axis=-1, keepdims=True)
            o = (_dot(e_c.astype(BF16), vc_ref[b, rows_h, :].astype(BF16))
                 + _dot(e_n.astype(BF16), vn_ref[rs, hs]))
            o_ref[rs, hs] = (o * (1.0 / den)).astype(BF16)


def _attn_s_call(q, k, v, cache_k, cache_v, toep, l):
    nb, rows = SEQS_PER_STEP, SEQS_PER_STEP * DEC_SEQ
    first = ROWS_P // rows
    new = pl.BlockSpec((rows, ATTN_DIM), lambda i: (first + i, 0))
    cache = pl.BlockSpec((None, nb, ATTN_REACH * N_HEADS, HEAD_DIM), lambda i: (l, i, 0, 0))
    return _call(
        _attn_s_body, (DEC_BATCH // nb,),
        [new, new, new, cache, cache, _resident((None, N_HEADS, TOEP), (l, 0, 0))],
        pl.BlockSpec((rows, ATTN_DIM), lambda i: (i, 0)),
        jax.ShapeDtypeStruct((ROWS_S, ATTN_DIM), BF16),
        scratch=[pltpu.VMEM((N_HEADS, DEC_SEQ, ATTN_REACH + DEC_SEQ), F32)],
        name="attn_sample", vmem_mb=40)(q, k, v, cache_k, cache_v, toep)


def _toeplitz_table(rel_tab):
    left = 3 * QB - 1 - REL_CLIP
    right = TOEP - left - (2 * REL_CLIP + 1)
    pad = [(0, 0)] * (rel_tab.ndim - 1) + [(left, right)]
    return jnp.pad(rel_tab[..., ::-1] * LOG2E, pad, mode="edge").astype(F32)


def kernel(x_prompt, x_sample, p_prompt, p_sample, cache_conv, cache_k, cache_v, norm_mix, w_in, conv_w, conv_b, conv_ln_g, conv_ln_b, w_a_out, gmlp_ln_g, gmlp_ln_b, gmlp_ws, gmlp_bs, w_b_out, attn_rel_bias, w_c_out, w_o, norm_ffn, w_ff1, w_ff2, norm_ple, w_ple_gate, w_ple_proj, norm_final):
    p_p = p_prompt.reshape(DEPTH, ROWS_P, PLE_DIM)
    p_s = p_sample.reshape(DEPTH, ROWS_S, PLE_DIM)
    cache_k = cache_k.reshape(DEPTH, DEC_BATCH, ATTN_REACH * N_HEADS, HEAD_DIM)
    cache_v = cache_v.reshape(DEPTH, DEC_BATCH, ATTN_REACH * N_HEADS, HEAD_DIM)
    toep = _toeplitz_table(attn_rel_bias)

    vec = lambda x: x.reshape(x.shape[0], 1, x.shape[-1])
    g_mix = vec(jnp.concatenate([norm_mix, norm_final[None]], axis=0))
    conv_b3, conv_g3, conv_lb3 = vec(conv_b), vec(conv_ln_g), vec(conv_ln_b)
    gm_g3, gm_b3 = vec(gmlp_ln_g), vec(gmlp_ln_b)
    g_ffn3, g_ple3 = vec(norm_ffn), vec(norm_ple)

    outs = [[] for _ in range(7)]
    h, xn = _first_call(x_prompt.reshape(ROWS_P, D_MODEL), x_sample.reshape(ROWS_S, D_MODEL), g_mix)
    for l in range(DEPTH):
        q, k, v, k_rows, v_rows = _qkv_call(xn, w_in, l)
        gates = _gate_call(xn, w_in, l)

        ws = gmlp_ws[l]
        corner = jnp.tile(ws[:, :DEC_SEQ, :DEC_SEQ], (1, GMLP_CHUNK // DEC_SEQ, GMLP_CHUNK // DEC_SEQ))
        bias_p = jnp.repeat(gmlp_bs[l].T, GMLP_GC, axis=1)
        bias_s = jnp.tile(bias_p[:DEC_SEQ], (GMLP_CHUNK // DEC_SEQ, 1))
        a, yb_in, v_s = _glu_gmlp_call(xn, w_in, gm_g3, gm_b3, jnp.stack([ws, corner]),
                                       jnp.stack([bias_p, bias_s]), l)
        ya_in, w_ff1_b, w_ff2_b = _conv_call(a, cache_conv, conv_w, conv_b3, conv_g3, conv_lb3, w_ff1, w_ff2, l)

        o_p = _attn_p_call(q, k, v, toep, l)
        o_s = _attn_s_call(q, k, v, cache_k, cache_v, toep, l)

        h1, hn = _merge_call(ya_in, yb_in, o_p, o_s, gates, h, w_a_out, w_b_out, w_c_out, w_o, g_ffn3, l)
        h2 = _ffn_call(hn, h1, w_ff1_b, w_ff2_b)
        last = l == DEPTH - 1
        h, xn = _ple_call(h2, p_p, p_s, w_ple_gate, w_ple_proj, g_ple3, g_mix, l, last)

        outs[0].append(jnp.stack([a[(b + 1) * SEQ - CONV_HIST:(b + 1) * SEQ] for b in range(BATCH)]))
        outs[1].append(a[ROWS_P:].reshape(DEC_BATCH, DEC_SEQ, CONV_DIM)[:, DEC_SEQ - CONV_HIST:])
        for j, rows in enumerate((k_rows, v_rows)):
            rows = rows.reshape(KV_SLOTS, TM, N_HEADS, HEAD_DIM)
            outs[2 + j].append(rows[:BATCH])
            outs[4 + j].append(rows[BATCH].reshape(DEC_BATCH, DEC_SEQ, N_HEADS, HEAD_DIM))
        outs[6].append(v_s.reshape(DEC_BATCH, DEC_SEQ, GMLP_DIM))

    y_prompt = h.reshape(BATCH, SEQ, D_MODEL)
    y_sample = xn.reshape(DEC_BATCH, DEC_SEQ, D_MODEL)
    return (y_prompt, y_sample) + tuple(jnp.stack(x) for x in outs)
```

```python
import functools

import jax
import jax.numpy as jnp
from jax import lax
from jax.experimental import pallas as pl
from jax.experimental.pallas import tpu as pltpu

F32 = jnp.float32
BF16 = jnp.bfloat16

D_MODEL = 2048
BATCH = 2
SEQ = 4096
DEPTH = 4
DEC_BATCH = 16
DEC_SEQ = 32
CHUNK = 64
CONV_DIM = 1024
CONV_WIDTH = 31
CONV_HIST = CONV_WIDTH - 1
GMLP_DIM = 1024
GMLP_GROUPS = 8
GMLP_GC = GMLP_DIM // GMLP_GROUPS
GMLP_CHUNK = 128
N_HEADS = 8
HEAD_DIM = 128
ATTN_DIM = N_HEADS * HEAD_DIM
LEFT_CHUNKS = 8
ATTN_REACH = LEFT_CHUNKS * CHUNK
REL_CLIP = 128
D_FF = 4 * D_MODEL
PLE_DIM = 256
EPS = 1e-6
SCALE = HEAD_DIM ** -0.5
LOG2E = 1.4426950408889634
NEG = -1e30

ROWS_P = BATCH * SEQ
ROWS_S = DEC_BATCH * DEC_SEQ
ROWS = ROWS_P + ROWS_S
TM = 512
NT = ROWS // TM
NT_P = ROWS_P // TM
TILES_PER_SEQ = SEQ // TM
TM_MERGE = 256
TF = 2048
COLB = 1024
QB = 256
KB = 3 * QB
QPS = 2
CONV_RC = 256
CONV_LB = 128
SUBLANES = 8
TOEP = 1024
KW = 8
KW_WIDE = 8
SEQS_PER_STEP = 2
MIB = 1 << 20


def _rms(x, g):
    return x * lax.rsqrt(jnp.mean(x * x, axis=-1, keepdims=True) + EPS) * g


def _ln(x, g, b):
    mu = jnp.mean(x, axis=-1, keepdims=True)
    xc = x - mu
    var = jnp.mean(xc * xc, axis=-1, keepdims=True)
    return xc * lax.rsqrt(var + EPS) * g + b


def _dot(a, b):
    return jnp.dot(a, b, preferred_element_type=F32)


def _dot_nt(a, b):
    return lax.dot_general(a, b, (((1,), (1,)), ((), ())), preferred_element_type=F32)


def _call(body, grid, in_specs, out_specs, out_shape, *, name, vmem_mb, scratch=()):
    return pl.pallas_call(
        body,
        grid=grid,
        in_specs=in_specs,
        out_specs=out_specs,
        out_shape=out_shape,
        scratch_shapes=list(scratch),
        compiler_params=pltpu.CompilerParams(
            dimension_semantics=("arbitrary",) * len(grid), vmem_limit_bytes=vmem_mb * MIB),
        name=name,
    )


def _rows(cols, tm=TM, cb=0, lag=0):
    return pl.BlockSpec((tm, cols), lambda s, *_: (jnp.maximum(s - lag, 0), cb))


def _resident(block, index):
    return pl.BlockSpec(block, lambda *_: index, pipeline_mode=pl.Buffered(1))


def _vec(l, n):
    return _resident((None, 1, n), (l, 0, 0))


def _wchunk(l, k, cols, kw, cb=0):
    return pl.BlockSpec((None, k // kw, cols), lambda s, *_: (l, jnp.minimum(s, kw - 1), cb))


def _wcopy(k, cols):
    return pltpu.VMEM((k, cols), BF16)


def _stage_weights(chunk_refs, w_refs, kw):
    s = pl.program_id(0)

    @pl.when(s < kw)
    def _():
        for c_ref, w_ref in zip(chunk_refs, w_refs):
            rows = c_ref.shape[0]
            w_ref[pl.ds(pl.multiple_of(s * rows, rows), rows), :] = c_ref[...].astype(BF16)

    return s >= kw


def _group_specs(cols):
    assert ROWS_S == TM
    return [pl.BlockSpec((TM, cols), lambda i: (jnp.minimum(i, NT_P - 1), 0)),
            pl.BlockSpec((TM, cols), lambda i: (0, 0))]


def _first_body(xp_ref, xs_ref, g_ref, h_ref, xn_ref):
    x = jnp.where(pl.program_id(0) == NT_P, xs_ref[...], xp_ref[...])
    h_ref[...] = x
    xn_ref[...] = _rms(x, g_ref[...]).astype(BF16)


def _first_call(xp, xs, g3):
    return _call(
        _first_body, (NT,), _group_specs(D_MODEL) + [_vec(0, D_MODEL)], [_rows(D_MODEL)] * 2,
        [jax.ShapeDtypeStruct((ROWS, D_MODEL), F32), jax.ShapeDtypeStruct((ROWS, D_MODEL), BF16)],
        name="rms_first", vmem_mb=40)(xp, xs, g3)


def _glu_gmlp_body(x_ref, ca_ref, cg_ref, cu_ref, cv_ref, g_ref, b_ref, ws_ref, bias_ref,
                   a_ref, yb_ref, vs_ref, wa_ref, wg_ref, wu_ref, wv_ref):
    main = _stage_weights((ca_ref, cg_ref, cu_ref, cv_ref), (wa_ref, wg_ref, wu_ref, wv_ref), KW_WIDE)

    @pl.when(main)
    def _():
        x = x_ref[...]
        v = _ln(jax.nn.gelu(_dot(x, wv_ref[...])), g_ref[...], b_ref[...])
        vs_ref[...] = v
        u = jax.nn.gelu(_dot(x, wu_ref[...]))
        a_ref[...] = _dot(x, wa_ref[...]) * jax.nn.sigmoid(_dot(x, wg_ref[...]))
        shift = jnp.where(pl.program_id(0) - KW_WIDE == NT_P, 5, 7)
        r = lax.broadcasted_iota(jnp.int32, (GMLP_CHUNK, GMLP_CHUNK), 0)
        c = lax.broadcasted_iota(jnp.int32, (GMLP_CHUNK, GMLP_CHUNK), 1)
        mask = (r >= c) & (lax.shift_right_logical(r, shift) == lax.shift_right_logical(c, shift))
        for g in range(GMLP_GROUPS):
            cs = slice(g * GMLP_GC, (g + 1) * GMLP_GC)
            wg = jnp.where(mask, ws_ref[g], 0.0).astype(BF16)
            bias = bias_ref[:, cs]
            for k in range(TM // GMLP_CHUNK):
                rs = slice(k * GMLP_CHUNK, (k + 1) * GMLP_CHUNK)
                mixed = _dot(wg, v[rs, cs].astype(BF16)) + bias
                yb_ref[rs, cs] = (u[rs, cs] * mixed).astype(BF16)


def _glu_gmlp_call(xn, w_in, ln_g, ln_b, wmix, bmix, l):
    assert ROWS_S == TM
    kw = KW_WIDE
    pick = lambda s: (jnp.where(s - kw == NT_P, 1, 0), 0, 0, 0)
    return _call(
        _glu_gmlp_body, (kw + NT,),
        [_rows(D_MODEL, lag=kw)] + [_wchunk(l, D_MODEL, COLB, kw, j) for j in range(4)]
        + [_vec(l, GMLP_DIM), _vec(l, GMLP_DIM),
           pl.BlockSpec((None, GMLP_GROUPS, GMLP_CHUNK, GMLP_CHUNK), pick),
           pl.BlockSpec((None, GMLP_CHUNK, GMLP_DIM), lambda s: pick(s)[:3])],
        [_rows(CONV_DIM, lag=kw), _rows(GMLP_DIM, lag=kw), pl.BlockSpec((TM, GMLP_DIM), lambda s: (0, 0))],
        [jax.ShapeDtypeStruct((ROWS, CONV_DIM), F32), jax.ShapeDtypeStruct((ROWS, GMLP_DIM), BF16),
         jax.ShapeDtypeStruct((ROWS_S, GMLP_DIM), F32)],
        scratch=[_wcopy(D_MODEL, COLB)] * 4,
        name="inproj_glu_gmlp", vmem_mb=58)(xn, w_in, w_in, w_in, w_in, ln_g, ln_b, wmix, bmix)


KV_SLOTS = BATCH + 1


def _qkv_body(x_ref, cq_ref, ck_ref, cv_ref, q_ref, k_ref, v_ref, kr_ref, vr_ref, wq_ref, wk_ref, wv_ref):
    @pl.when(_stage_weights((cq_ref, ck_ref, cv_ref), (wq_ref, wk_ref, wv_ref), KW))
    def _():
        x = x_ref[...]
        q_ref[...] = (_dot(x, wq_ref[...]) * (SCALE * LOG2E)).astype(BF16)
        for w_ref, o_ref, rows_ref in ((wk_ref, k_ref, kr_ref), (wv_ref, v_ref, vr_ref)):
            y = _dot(x, w_ref[...])
            o_ref[...] = y.astype(BF16)
            for h in range(N_HEADS):
                rows_ref[pl.ds(h, TM, stride=N_HEADS), :] = y[:, h * HEAD_DIM:(h + 1) * HEAD_DIM]


def _qkv_call(xn, w_in, l):
    act = jax.ShapeDtypeStruct((ROWS, ATTN_DIM), BF16)
    rows = jax.ShapeDtypeStruct((KV_SLOTS * TM * N_HEADS, HEAD_DIM), F32)
    kv_slot = lambda s: (jnp.maximum(s - KW, 0) // TILES_PER_SEQ, 0)
    return _call(
        _qkv_body, (KW + NT,),
        [_rows(D_MODEL, lag=KW)] + [_wchunk(l, D_MODEL, COLB, KW, 4 + j) for j in range(3)],
        [_rows(ATTN_DIM, lag=KW)] * 3 + [pl.BlockSpec((TM * N_HEADS, HEAD_DIM), kv_slot)] * 2,
        [act, act, act, rows, rows],
        scratch=[_wcopy(D_MODEL, COLB)] * 3,
        name="inproj_qkv", vmem_mb=52)(xn, w_in, w_in, w_in)


N_GATE_BLOCKS = 3 * D_MODEL // COLB


def _gate_body(x_ref, *refs):
    n = N_GATE_BLOCKS
    c_refs, o_ref, w_refs = refs[:n], refs[n], refs[n + 1:]

    @pl.when(_stage_weights(c_refs, w_refs, KW_WIDE))
    def _():
        x = x_ref[...]
        for j, w_ref in enumerate(w_refs):
            o_ref[:, j * COLB:(j + 1) * COLB] = jax.nn.sigmoid(_dot(x, w_ref[...])).astype(BF16)


def _gate_call(xn, w_in, l):
    n = N_GATE_BLOCKS
    return _call(
        _gate_body, (KW_WIDE + NT,),
        [_rows(D_MODEL, lag=KW_WIDE)] + [_wchunk(l, D_MODEL, COLB, KW_WIDE, 7 + j) for j in range(n)],
        _rows(3 * D_MODEL, lag=KW_WIDE), jax.ShapeDtypeStruct((ROWS, 3 * D_MODEL), BF16),
        scratch=[_wcopy(D_MODEL, COLB)] * n,
        name="inproj_gates", vmem_mb=56)(xn, *([w_in] * n))


def _merge_body(a_ref, b_ref, cp_ref, cs_ref, g0_ref, g1_ref, g2_ref, h_ref,
                ca_ref, cb_ref, cc_ref, co_ref, gf_ref, h1_ref, hn_ref,
                wa_ref, wb_ref, wc_ref, wo_ref):
    main = _stage_weights((ca_ref, cb_ref, cc_ref, co_ref), (wa_ref, wb_ref, wc_ref, wo_ref), KW_WIDE)

    @pl.when(main)
    def _():
        is_sample = pl.program_id(0) - KW_WIDE >= ROWS_P // TM_MERGE
        c = jnp.where(is_sample, cs_ref[...], cp_ref[...])
        m = g0_ref[...].astype(F32) * _dot(a_ref[...], wa_ref[...])
        m = m + g1_ref[...].astype(F32) * _dot(b_ref[...], wb_ref[...])
        m = m + g2_ref[...].astype(F32) * _dot(c, wc_ref[...])
        h1 = h_ref[...] + _dot(m.astype(BF16), wo_ref[...])
        h1_ref[...] = h1
        hn_ref[...] = _rms(h1, gf_ref[...]).astype(BF16)


def _merge_call(a, b, c_p, c_s, gates, h, w_a, w_b, w_c, w_o, g_ffn, l):
    tm, kw = TM_MERGE, KW_WIDE
    np_ = ROWS_P // tm
    rows = functools.partial(_rows, tm=tm, lag=kw)
    return _call(
        _merge_body, (kw + ROWS // tm,),
        [rows(CONV_DIM), rows(GMLP_DIM),
         pl.BlockSpec((tm, ATTN_DIM), lambda s: (jnp.clip(s - kw, 0, np_ - 1), 0)),
         pl.BlockSpec((tm, ATTN_DIM), lambda s: (jnp.maximum(s - kw - np_, 0), 0)),
         rows(D_MODEL, cb=0), rows(D_MODEL, cb=1), rows(D_MODEL, cb=2), rows(D_MODEL),
         _wchunk(l, CONV_DIM, D_MODEL, kw), _wchunk(l, GMLP_DIM, D_MODEL, kw),
         _wchunk(l, ATTN_DIM, D_MODEL, kw), _wchunk(l, D_MODEL, D_MODEL, kw), _vec(l, D_MODEL)],
        [rows(D_MODEL), rows(D_MODEL)],
        [jax.ShapeDtypeStruct((ROWS, D_MODEL), F32), jax.ShapeDtypeStruct((ROWS, D_MODEL), BF16)],
        scratch=[_wcopy(CONV_DIM, D_MODEL), _wcopy(GMLP_DIM, D_MODEL), _wcopy(ATTN_DIM, D_MODEL),
                 _wcopy(D_MODEL, D_MODEL)],
        name="merge_out", vmem_mb=58)(a, b, c_p, c_s, gates, gates, gates, h, w_a, w_b, w_c, w_o, g_ffn)


def _ffn_body(hn_ref, w1_ref, w2_ref, h1_ref, o_ref):
    def delta():
        a = jnp.square(jnp.maximum(_dot(hn_ref[...], w1_ref[...]), 0.0)).astype(BF16)
        return _dot(a, w2_ref[...])

    @pl.when(pl.program_id(1) == 0)
    def _():
        o_ref[...] = h1_ref[...] + delta()

    @pl.when(pl.program_id(1) != 0)
    def _():
        o_ref[...] += delta()


def _ffn_call(hn, h1, w1, w2):
    return _call(
        _ffn_body, (NT, D_FF // TF),
        [pl.BlockSpec((TM, D_MODEL), lambda i, f: (i, 0)),
         pl.BlockSpec((D_MODEL, TF), lambda i, f: (0, f)),
         pl.BlockSpec((TF, D_MODEL), lambda i, f: (f, 0)),
         pl.BlockSpec((TM, D_MODEL), lambda i, f: (i, 0))],
        pl.BlockSpec((TM, D_MODEL), lambda i, f: (i, 0)),
        jax.ShapeDtypeStruct((ROWS, D_MODEL), F32),
        name="ffn", vmem_mb=62)(hn, w1, w2, h1)


def _ple_update(tile, h2_ref, pp_ref, ps_ref, wg_ref, wp_ref, gp_ref):
    h2 = h2_ref[...]
    p = jnp.where(tile == NT_P, ps_ref[...], pp_ref[...])
    gate = jax.nn.sigmoid(_dot(_rms(h2, gp_ref[...]).astype(BF16), wg_ref[...]))
    return h2 + gate * _dot(p.astype(BF16), wp_ref[...])


def _ple_body(h2_ref, pp_ref, ps_ref, cg_ref, cp_ref, gp_ref, gn_ref, h3_ref, xn_ref, wg_ref, wp_ref):
    @pl.when(_stage_weights((cg_ref, cp_ref), (wg_ref, wp_ref), KW))
    def _():
        h3 = _ple_update(pl.program_id(0) - KW, h2_ref, pp_ref, ps_ref, wg_ref, wp_ref, gp_ref)
        h3_ref[...] = h3
        xn_ref[...] = _rms(h3, gn_ref[...]).astype(BF16)


def _ple_last_body(h2_ref, pp_ref, ps_ref, cg_ref, cp_ref, gp_ref, gn_ref, yp_ref, ys_ref, wg_ref, wp_ref):
    main = _stage_weights((cg_ref, cp_ref), (wg_ref, wp_ref), KW)
    tile = pl.program_id(0) - KW

    def final():
        return _rms(_ple_update(tile, h2_ref, pp_ref, ps_ref, wg_ref, wp_ref, gp_ref), gn_ref[...])

    @pl.when(main & (tile < NT_P))
    def _():
        yp_ref[...] = final()

    @pl.when(tile >= NT_P)
    def _():
        ys_ref[...] = final()


def _ple_call(h2, p_p, p_s, w_g, w_p, g_ple, g_next, l, last):
    assert ROWS_S == TM
    ins = [_rows(D_MODEL, lag=KW),
           pl.BlockSpec((None, TM, PLE_DIM), lambda s: (l, jnp.clip(s - KW, 0, NT_P - 1), 0)),
           pl.BlockSpec((None, TM, PLE_DIM), lambda s: (l, 0, 0)),
           _wchunk(l, D_MODEL, D_MODEL, KW), _wchunk(l, PLE_DIM, D_MODEL, KW),
           _vec(l, D_MODEL), _vec(l + 1, D_MODEL)]
    scratch = [_wcopy(D_MODEL, D_MODEL), _wcopy(PLE_DIM, D_MODEL)]
    args = (h2, p_p, p_s, w_g, w_p, g_ple, g_next)
    if not last:
        return _call(
            _ple_body, (KW + NT,), ins, [_rows(D_MODEL, lag=KW)] * 2,
            [jax.ShapeDtypeStruct((ROWS, D_MODEL), F32), jax.ShapeDtypeStruct((ROWS, D_MODEL), BF16)],
            scratch=scratch, name="ple", vmem_mb=56)(*args)
    return _call(
        _ple_last_body, (KW + NT,), ins,
        [pl.BlockSpec((TM, D_MODEL), lambda s: (jnp.clip(s - KW, 0, NT_P - 1), 0)),
         pl.BlockSpec((TM, D_MODEL), lambda s: (0, 0))],
        [jax.ShapeDtypeStruct((ROWS_P, D_MODEL), F32), jax.ShapeDtypeStruct((ROWS_S, D_MODEL), F32)],
        scratch=scratch, name="ple_final", vmem_mb=56)(*args)


def _conv_rows(ext_ref, y_ref, cw_ref, ext_base, y_base, rc):
    for lb in range(CONV_DIM // CONV_LB):
        ls = slice(lb * CONV_LB, (lb + 1) * CONV_LB)
        acc = None
        for r in range(SUBLANES):
            rows = rc + (SUBLANES if r else 0)
            u = None
            for j in range(CONV_WIDTH):
                if (2 + j) % SUBLANES == r:
                    start = ext_base + (2 + j - r)
                    if not isinstance(start, int):
                        start = pl.multiple_of(start, SUBLANES)
                    term = cw_ref[j:j + 1, ls] * ext_ref[pl.ds(start, rows), ls]
                    u = term if u is None else u + term
            part = u[r:r + rc, :]
            acc = part if acc is None else acc + part
        y_ref[pl.ds(y_base, rc), ls] = acc


def _conv_body(a_ref, cache_ref, cw_ref, cb_ref, g_ref, b_ref, f1_ref, f2_ref,
               o_ref, f1b_ref, f2b_ref, ext_ref, y_ref):
    i = pl.program_id(0)
    hist = 32

    f1b_ref[...] = f1_ref[...].astype(BF16)
    f2b_ref[...] = f2_ref[...].astype(BF16)

    @pl.when(i < NT_P)
    def _():
        @pl.when(i % TILES_PER_SEQ == 0)
        def _():
            ext_ref[0:hist, :] = jnp.zeros((hist, CONV_DIM), F32)

        @pl.when(i % TILES_PER_SEQ != 0)
        def _():
            ext_ref[0:hist, :] = ext_ref[TM:TM + hist, :]

        ext_ref[hist:hist + TM, :] = a_ref[...]

        def chunk(c, carry):
            r0 = pl.multiple_of(c * CONV_RC, CONV_RC)
            _conv_rows(ext_ref, y_ref, cw_ref, r0, r0, CONV_RC)
            return carry

        lax.fori_loop(0, TM // CONV_RC, chunk, 0)

    @pl.when(i == NT_P)
    def _():
        def seq(s, carry):
            r0 = pl.multiple_of(s * DEC_SEQ, DEC_SEQ)
            ext_ref[hist - CONV_HIST:hist, :] = cache_ref[s]
            ext_ref[hist:hist + DEC_SEQ, :] = a_ref[pl.ds(r0, DEC_SEQ), :]
            _conv_rows(ext_ref, y_ref, cw_ref, 0, r0, DEC_SEQ)
            return carry

        lax.fori_loop(0, DEC_BATCH, seq, 0)

    y = _ln(y_ref[...] + cb_ref[...], g_ref[...], b_ref[...])
    o_ref[...] = (y * jax.nn.sigmoid(y)).astype(BF16)


FF_SLABS = NT - 1


def _conv_call(a, cache_conv, conv_w, conv_b, ln_g, ln_b, w_ff1, w_ff2, l):
    r1, r2 = D_MODEL // FF_SLABS, D_FF // FF_SLABS
    slab = lambda i: jnp.minimum(i, FF_SLABS - 1)
    return _call(
        _conv_body, (NT,),
        [_rows(CONV_DIM),
         _resident((None, DEC_BATCH, CONV_HIST, CONV_DIM), (l, 0, 0, 0)),
         _resident((None, CONV_WIDTH, CONV_DIM), (l, 0, 0)),
         _vec(l, CONV_DIM), _vec(l, CONV_DIM), _vec(l, CONV_DIM),
         pl.BlockSpec((None, r1, D_FF), lambda i: (l, slab(i), 0)),
         pl.BlockSpec((None, r2, D_MODEL), lambda i: (l, slab(i), 0))],
        [_rows(CONV_DIM), pl.BlockSpec((r1, D_FF), lambda i: (slab(i), 0)),
         pl.BlockSpec((r2, D_MODEL), lambda i: (slab(i), 0))],
        [jax.ShapeDtypeStruct((ROWS, CONV_DIM), BF16), jax.ShapeDtypeStruct((D_MODEL, D_FF), BF16),
         jax.ShapeDtypeStruct((D_FF, D_MODEL), BF16)],
        scratch=[pltpu.VMEM((TM + 32, CONV_DIM), F32), pltpu.VMEM((TM, CONV_DIM), F32)],
        name="conv_mixer", vmem_mb=48)(a, cache_conv, conv_w, conv_b, ln_g, ln_b, w_ff1, w_ff2)


def _toeplitz_rows(tab_ref, h, rows):
    t = jnp.broadcast_to(tab_ref[h:h + 1, :], (rows, TOEP))
    return pltpu.roll(t, TOEP - QB + 1, 1, stride=1, stride_axis=0)


def _attn_p_body(q_ref, *refs):
    nkb = QPS + 2
    k_refs, v_refs = refs[:nkb], refs[nkb:2 * nkb]
    tab_ref, o_ref, bias_ref = refs[2 * nkb:]
    m = pl.program_id(1)

    @pl.when((pl.program_id(0) == 0) & (m == 0))
    def _():
        r = lax.broadcasted_iota(jnp.int32, (QB, KB), 0) // CHUNK
        j = lax.broadcasted_iota(jnp.int32, (QB, KB), 1) // CHUNK
        band = (j >= r) & (j <= r + LEFT_CHUNKS)
        for h in range(N_HEADS):
            bias = jnp.where(band, _toeplitz_rows(tab_ref, h, QB)[:, :KB], NEG)
            for s in range(3):
                bias_ref[h, s] = bias[:, s * QB:(s + 1) * QB]
            bias_ref[h, 3] = jnp.full((QB, QB), NEG, F32)

    for j in range(QPS):
        rs = slice(j * QB, (j + 1) * QB)
        slot = [jnp.where(QPS * m + j + s - 2 >= 0, s, 3) for s in range(3)]
        for h in range(N_HEADS):
            hs = slice(h * HEAD_DIM, (h + 1) * HEAD_DIM)
            qh = q_ref[rs, hs]
            sc = [_dot_nt(qh, k_refs[j + s][:, hs]) + bias_ref[h, slot[s]] for s in range(3)]
            mx = jnp.max(functools.reduce(jnp.maximum, sc), axis=-1, keepdims=True)
            e = [jnp.exp2(x - mx) for x in sc]
            den = jnp.sum(functools.reduce(jnp.add, e), axis=-1, keepdims=True)
            o = functools.reduce(jnp.add, [_dot(e[s].astype(BF16), v_refs[j + s][:, hs]) for s in range(3)])
            o_ref[rs, hs] = (o * (1.0 / den)).astype(BF16)


def _attn_p_call(q, k, v, toep, l):
    nq = SEQ // QB
    steps = nq // QPS
    nkb = QPS + 2

    def kv_spec(t):
        return pl.BlockSpec((QB, ATTN_DIM), lambda b, m: (b * nq + jnp.maximum(QPS * m - 2 + t, 0), 0))

    q_spec = pl.BlockSpec((QPS * QB, ATTN_DIM), lambda b, m: (b * steps + m, 0))
    return _call(
        _attn_p_body, (BATCH, steps),
        [q_spec] + [kv_spec(t) for t in range(nkb)] * 2 + [_resident((None, N_HEADS, TOEP), (l, 0, 0))],
        q_spec, jax.ShapeDtypeStruct((ROWS_P, ATTN_DIM), BF16),
        scratch=[pltpu.VMEM((N_HEADS, 4, QB, QB), F32)],
        name="attn_prompt", vmem_mb=48)(q, *([k] * nkb), *([v] * nkb), toep)


def _attn_s_body(q_ref, kn_ref, vn_ref, kc_ref, vc_ref, tab_ref, o_ref, bias_ref):
    w = ATTN_REACH

    @pl.when(pl.program_id(0) == 0)
    def _():
        for h in range(N_HEADS):
            bias_ref[h] = _toeplitz_rows(tab_ref, h, DEC_SEQ)[:, :w + DEC_SEQ]

    for b in range(SEQS_PER_STEP):
        rs = slice(b * DEC_SEQ, (b + 1) * DEC_SEQ)
        for h in range(N_HEADS):
            hs = slice(h * HEAD_DIM, (h + 1) * HEAD_DIM)
            rows_h = pl.ds(h, w, stride=N_HEADS)
            qh = q_ref[rs, hs]
            s_c = _dot_nt(qh, kc_ref[b, rows_h, :].astype(BF16)) + bias_ref[h, :, :w]
            s_n = _dot_nt(qh, kn_ref[rs, hs]) + bias_ref[h, :, w:]
            mx = jnp.maximum(jnp.max(s_c, axis=-1, keepdims=True), jnp.max(s_n, axis=-1, keepdims=True))
            e_c = jnp.exp2(s_c - mx)
            e_n = jnp.exp2(s_n - mx)
            den = jnp.sum(e_c, axis=-1, keepdims=True) + jnp.sum(e_n, axis=-1, keepdims=True)
            o = (_dot(e_c.astype(BF16), vc_ref[b, rows_h, :].astype(BF16))
                 + _dot(e_n.astype(BF16), vn_ref[rs, hs]))
            o_ref[rs, hs] = (o * (1.0 / den)).astype(BF16)


def _attn_s_call(q, k, v, cache_k, cache_v, toep, l):
    nb, rows = SEQS_PER_STEP, SEQS_PER_STEP * DEC_SEQ
    first = ROWS_P // rows
    new = pl.BlockSpec((rows, ATTN_DIM), lambda i: (first + i, 0))
    cache = pl.BlockSpec((None, nb, ATTN_REACH * N_HEADS, HEAD_DIM), lambda i: (l, i, 0, 0))
    return _call(
        _attn_s_body, (DEC_BATCH // nb,),
        [new, new, new, cache, cache, _resident((None, N_HEADS, TOEP), (l, 0, 0))],
        pl.BlockSpec((rows, ATTN_DIM), lambda i: (i, 0)),
        jax.ShapeDtypeStruct((ROWS_S, ATTN_DIM), BF16),
        scratch=[pltpu.VMEM((N_HEADS, DEC_SEQ, ATTN_REACH + DEC_SEQ), F32)],
        name="attn_sample", vmem_mb=40)(q, k, v, cache_k, cache_v, toep)


def _toeplitz_table(rel_tab):
    left = 3 * QB - 1 - REL_CLIP
    right = TOEP - left - (2 * REL_CLIP + 1)
    pad = [(0, 0)] * (rel_tab.ndim - 1) + [(left, right)]
    return jnp.pad(rel_tab[..., ::-1] * LOG2E, pad, mode="edge").astype(F32)


def kernel(x_prompt, x_sample, p_prompt, p_sample, cache_conv, cache_k, cache_v, norm_mix, w_in, conv_w, conv_b, conv_ln_g, conv_ln_b, w_a_out, gmlp_ln_g, gmlp_ln_b, gmlp_ws, gmlp_bs, w_b_out, attn_rel_bias, w_c_out, w_o, norm_ffn, w_ff1, w_ff2, norm_ple, w_ple_gate, w_ple_proj, norm_final):
    p_p = p_prompt.reshape(DEPTH, ROWS_P, PLE_DIM)
    p_s = p_sample.reshape(DEPTH, ROWS_S, PLE_DIM)
    cache_k = cache_k.reshape(DEPTH, DEC_BATCH, ATTN_REACH * N_HEADS, HEAD_DIM)
    cache_v = cache_v.reshape(DEPTH, DEC_BATCH, ATTN_REACH * N_HEADS, HEAD_DIM)
    toep = _toeplitz_table(attn_rel_bias)

    vec = lambda x: x.reshape(x.shape[0], 1, x.shape[-1])
    g_mix = vec(jnp.concatenate([norm_mix, norm_final[None]], axis=0))
    conv_b3, conv_g3, conv_lb3 = vec(conv_b), vec(conv_ln_g), vec(conv_ln_b)
    gm_g3, gm_b3 = vec(gmlp_ln_g), vec(gmlp_ln_b)
    g_ffn3, g_ple3 = vec(norm_ffn), vec(norm_ple)

    outs = [[] for _ in range(7)]
    h, xn = _first_call(x_prompt.reshape(ROWS_P, D_MODEL), x_sample.reshape(ROWS_S, D_MODEL), g_mix)
    for l in range(DEPTH):
        q, k, v, k_rows, v_rows = _qkv_call(xn, w_in, l)
        gates = _gate_call(xn, w_in, l)

        ws = gmlp_ws[l]
        corner = jnp.tile(ws[:, :DEC_SEQ, :DEC_SEQ], (1, GMLP_CHUNK // DEC_SEQ, GMLP_CHUNK // DEC_SEQ))
        bias_p = jnp.repeat(gmlp_bs[l].T, GMLP_GC, axis=1)
        bias_s = jnp.tile(bias_p[:DEC_SEQ], (GMLP_CHUNK // DEC_SEQ, 1))
        a, yb_in, v_s = _glu_gmlp_call(xn, w_in, gm_g3, gm_b3, jnp.stack([ws, corner]),
                                       jnp.stack([bias_p, bias_s]), l)
        ya_in, w_ff1_b, w_ff2_b = _conv_call(a, cache_conv, conv_w, conv_b3, conv_g3, conv_lb3, w_ff1, w_ff2, l)

        o_p = _attn_p_call(q, k, v, toep, l)
        o_s = _attn_s_call(q, k, v, cache_k, cache_v, toep, l)

        h1, hn = _merge_call(ya_in, yb_in, o_p, o_s, gates, h, w_a_out, w_b_out, w_c_out, w_o, g_ffn3, l)
        h2 = _ffn_call(hn, h1, w_ff1_b, w_ff2_b)
        last = l == DEPTH - 1
        h, xn = _ple_call(h2, p_p, p_s, w_ple_gate, w_ple_proj, g_ple3, g_mix, l, last)

        outs[0].append(jnp.stack([a[(b + 1) * SEQ - CONV_HIST:(b + 1) * SEQ] for b in range(BATCH)]))
        outs[1].append(a[ROWS_P:].reshape(DEC_BATCH, DEC_SEQ, CONV_DIM)[:, DEC_SEQ - CONV_HIST:])
        for j, rows in enumerate((k_rows, v_rows)):
            rows = rows.reshape(KV_SLOTS, TM, N_HEADS, HEAD_DIM)
            outs[2 + j].append(rows[:BATCH])
            outs[4 + j].append(rows[BATCH].reshape(DEC_BATCH, DEC_SEQ, N_HEADS, HEAD_DIM))
        outs[6].append(v_s.reshape(DEC_BATCH, DEC_SEQ, GMLP_DIM))

    y_prompt = h.reshape(BATCH, SEQ, D_MODEL)
    y_sample = xn.reshape(DEC_BATCH, DEC_SEQ, D_MODEL)
    return (y_prompt, y_sample) + tuple(jnp.stack(x) for x in outs)
```

```python
import functools

import jax
import jax.numpy as jnp
from jax import lax
from jax.experimental import pallas as pl
from jax.experimental.pallas import tpu as pltpu

F32 = jnp.float32
BF16 = jnp.bfloat16

D_MODEL = 2048
BATCH = 2
SEQ = 4096
DEPTH = 4
DEC_BATCH = 16
DEC_SEQ = 32
CHUNK = 64
CONV_DIM = 1024
CONV_WIDTH = 31
CONV_HIST = CONV_WIDTH - 1
GMLP_DIM = 1024
GMLP_GROUPS = 8
GMLP_GC = GMLP_DIM // GMLP_GROUPS
GMLP_CHUNK = 128
N_HEADS = 8
HEAD_DIM = 128
ATTN_DIM = N_HEADS * HEAD_DIM
LEFT_CHUNKS = 8
ATTN_REACH = LEFT_CHUNKS * CHUNK
REL_CLIP = 128
D_FF = 4 * D_MODEL
PLE_DIM = 256
EPS = 1e-6
SCALE = HEAD_DIM ** -0.5
LOG2E = 1.4426950408889634
NEG = -1e30

ROWS_P = BATCH * SEQ
ROWS_S = DEC_BATCH * DEC_SEQ
ROWS = ROWS_P + ROWS_S
TM = 512
NT = ROWS // TM
NT_P = ROWS_P // TM
TILES_PER_SEQ = SEQ // TM
TM_MERGE = 256
TF = 2048
COLB = 1024
QB = 256
KB = 3 * QB
QPS = 2
CONV_RC = 256
CONV_LB = 128
SUBLANES = 8
TOEP = 1024
KW = 8
KW_WIDE = 8
SEQS_PER_STEP = 2
MIB = 1 << 20


def _rms(x, g):
    return x * lax.rsqrt(jnp.mean(x * x, axis=-1, keepdims=True) + EPS) * g


def _ln(x, g, b):
    mu = jnp.mean(x, axis=-1, keepdims=True)
    xc = x - mu
    var = jnp.mean(xc * xc, axis=-1, keepdims=True)
    return xc * lax.rsqrt(var + EPS) * g + b


def _dot(a, b):
    return jnp.dot(a, b, preferred_element_type=F32)


def _dot_nt(a, b):
    return lax.dot_general(a, b, (((1,), (1,)), ((), ())), preferred_element_type=F32)


def _call(body, grid, in_specs, out_specs, out_shape, *, name, vmem_mb, scratch=()):
    return pl.pallas_call(
        body,
        grid=grid,
        in_specs=in_specs,
        out_specs=out_specs,
        out_shape=out_shape,
        scratch_shapes=list(scratch),
        compiler_params=pltpu.CompilerParams(
            dimension_semantics=("arbitrary",) * len(grid), vmem_limit_bytes=vmem_mb * MIB),
        name=name,
    )


def _rows(cols, tm=TM, cb=0, lag=0):
    return pl.BlockSpec((tm, cols), lambda s, *_: (jnp.maximum(s - lag, 0), cb))


def _resident(block, index):
    return pl.BlockSpec(block, lambda *_: index, pipeline_mode=pl.Buffered(1))


def _vec(l, n):
    return _resident((None, 1, n), (l, 0, 0))


def _wchunk(l, k, cols, kw, cb=0):
    return pl.BlockSpec((None, k // kw, cols), lambda s, *_: (l, jnp.minimum(s, kw - 1), cb))


def _wcopy(k, cols):
    return pltpu.VMEM((k, cols), BF16)


def _stage_weights(chunk_refs, w_refs, kw):
    s = pl.program_id(0)

    @pl.when(s < kw)
    def _():
        for c_ref, w_ref in zip(chunk_refs, w_refs):
            rows = c_ref.shape[0]
            w_ref[pl.ds(pl.multiple_of(s * rows, rows), rows), :] = c_ref[...].astype(BF16)

    return s >= kw


def _group_specs(cols):
    assert ROWS_S == TM
    return [pl.BlockSpec((TM, cols), lambda i: (jnp.minimum(i, NT_P - 1), 0)),
            pl.BlockSpec((TM, cols), lambda i: (0, 0))]


def _first_body(xp_ref, xs_ref, g_ref, h_ref, xn_ref):
    x = jnp.where(pl.program_id(0) == NT_P, xs_ref[...], xp_ref[...])
    h_ref[...] = x
    xn_ref[...] = _rms(x, g_ref[...]).astype(BF16)


def _first_call(xp, xs, g3):
    return _call(
        _first_body, (NT,), _group_specs(D_MODEL) + [_vec(0, D_MODEL)], [_rows(D_MODEL)] * 2,
        [jax.ShapeDtypeStruct((ROWS, D_MODEL), F32), jax.ShapeDtypeStruct((ROWS, D_MODEL), BF16)],
        name="rms_first", vmem_mb=40)(xp, xs, g3)


def _glu_gmlp_body(x_ref, ca_ref, cg_ref, cu_ref, cv_ref, g_ref, b_ref, ws_ref, bias_ref,
                   a_ref, yb_ref, vs_ref, wa_ref, wg_ref, wu_ref, wv_ref):
    main = _stage_weights((ca_ref, cg_ref, cu_ref, cv_ref), (wa_ref, wg_ref, wu_ref, wv_ref), KW_WIDE)

    @pl.when(main)
    def _():
        x = x_ref[...]
        v = _ln(jax.nn.gelu(_dot(x, wv_ref[...])), g_ref[...], b_ref[...])
        vs_ref[...] = v
        u = jax.nn.gelu(_dot(x, wu_ref[...]))
        a_ref[...] = _dot(x, wa_ref[...]) * jax.nn.sigmoid(_dot(x, wg_ref[...]))
        shift = jnp.where(pl.program_id(0) - KW_WIDE == NT_P, 5, 7)
        r = lax.broadcasted_iota(jnp.int32, (GMLP_CHUNK, GMLP_CHUNK), 0)
        c = lax.broadcasted_iota(jnp.int32, (GMLP_CHUNK, GMLP_CHUNK), 1)
        mask = (r >= c) & (lax.shift_right_logical(r, shift) == lax.shift_right_logical(c, shift))
        for g in range(GMLP_GROUPS):
            cs = slice(g * GMLP_GC, (g + 1) * GMLP_GC)
            wg = jnp.where(mask, ws_ref[g], 0.0).astype(BF16)
            bias = bias_ref[:, cs]
            for k in range(TM // GMLP_CHUNK):
                rs = slice(k * GMLP_CHUNK, (k + 1) * GMLP_CHUNK)
                mixed = _dot(wg, v[rs, cs].astype(BF16)) + bias
                yb_ref[rs, cs] = (u[rs, cs] * mixed).astype(BF16)


def _glu_gmlp_call(xn, w_in, ln_g, ln_b, wmix, bmix, l):
    assert ROWS_S == TM
    kw = KW_WIDE
    pick = lambda s: (jnp.where(s - kw == NT_P, 1, 0), 0, 0, 0)
    return _call(
        _glu_gmlp_body, (kw + NT,),
        [_rows(D_MODEL, lag=kw)] + [_wchunk(l, D_MODEL, COLB, kw, j) for j in range(4)]
        + [_vec(l, GMLP_DIM), _vec(l, GMLP_DIM),
           pl.BlockSpec((None, GMLP_GROUPS, GMLP_CHUNK, GMLP_CHUNK), pick),
           pl.BlockSpec((None, GMLP_CHUNK, GMLP_DIM), lambda s: pick(s)[:3])],
        [_rows(CONV_DIM, lag=kw), _rows(GMLP_DIM, lag=kw), pl.BlockSpec((TM, GMLP_DIM), lambda s: (0, 0))],
        [jax.ShapeDtypeStruct((ROWS, CONV_DIM), F32), jax.ShapeDtypeStruct((ROWS, GMLP_DIM), BF16),
         jax.ShapeDtypeStruct((ROWS_S, GMLP_DIM), F32)],
        scratch=[_wcopy(D_MODEL, COLB)] * 4,
        name="inproj_glu_gmlp", vmem_mb=58)(xn, w_in, w_in, w_in, w_in, ln_g, ln_b, wmix, bmix)


KV_SLOTS = BATCH + 1


def _qkv_body(x_ref, cq_ref, ck_ref, cv_ref, q_ref, k_ref, v_ref, kr_ref, vr_ref, wq_ref, wk_ref, wv_ref):
    @pl.when(_stage_weights((cq_ref, ck_ref, cv_ref), (wq_ref, wk_ref, wv_ref), KW))
    def _():
        x = x_ref[...]
        q_ref[...] = (_dot(x, wq_ref[...]) * (SCALE * LOG2E)).astype(BF16)
        for w_ref, o_ref, rows_ref in ((wk_ref, k_ref, kr_ref), (wv_ref, v_ref, vr_ref)):
            y = _dot(x, w_ref[...])
            o_ref[...] = y.astype(BF16)
            for h in range(N_HEADS):
                rows_ref[pl.ds(h, TM, stride=N_HEADS), :] = y[:, h * HEAD_DIM:(h + 1) * HEAD_DIM]


def _qkv_call(xn, w_in, l):
    act = jax.ShapeDtypeStruct((ROWS, ATTN_DIM), BF16)
    rows = jax.ShapeDtypeStruct((KV_SLOTS * TM * N_HEADS, HEAD_DIM), F32)
    kv_slot = lambda s: (jnp.maximum(s - KW, 0) // TILES_PER_SEQ, 0)
    return _call(
        _qkv_body, (KW + NT,),
        [_rows(D_MODEL, lag=KW)] + [_wchunk(l, D_MODEL, COLB, KW, 4 + j) for j in range(3)],
        [_rows(ATTN_DIM, lag=KW)] * 3 + [pl.BlockSpec((TM * N_HEADS, HEAD_DIM), kv_slot)] * 2,
        [act, act, act, rows, rows],
        scratch=[_wcopy(D_MODEL, COLB)] * 3,
        name="inproj_qkv", vmem_mb=52)(xn, w_in, w_in, w_in)


N_GATE_BLOCKS = 3 * D_MODEL // COLB


def _gate_body(x_ref, *refs):
    n = N_GATE_BLOCKS
    c_refs, o_ref, w_refs = refs[:n], refs[n], refs[n + 1:]

    @pl.when(_stage_weights(c_refs, w_refs, KW_WIDE))
    def _():
        x = x_ref[...]
        for j, w_ref in enumerate(w_refs):
            o_ref[:, j * COLB:(j + 1) * COLB] = jax.nn.sigmoid(_dot(x, w_ref[...])).astype(BF16)


def _gate_call(xn, w_in, l):
    n = N_GATE_BLOCKS
    return _call(
        _gate_body, (KW_WIDE + NT,),
        [_rows(D_MODEL, lag=KW_WIDE)] + [_wchunk(l, D_MODEL, COLB, KW_WIDE, 7 + j) for j in range(n)],
        _rows(3 * D_MODEL, lag=KW_WIDE), jax.ShapeDtypeStruct((ROWS, 3 * D_MODEL), BF16),
        scratch=[_wcopy(D_MODEL, COLB)] * n,
        name="inproj_gates", vmem_mb=56)(xn, *([w_in] * n))


def _merge_body(a_ref, b_ref, cp_ref, cs_ref, g0_ref, g1_ref, g2_ref, h_ref,
                ca_ref, cb_ref, cc_ref, co_ref, gf_ref, h1_ref, hn_ref,
                wa_ref, wb_ref, wc_ref, wo_ref):
    main = _stage_weights((ca_ref, cb_ref, cc_ref, co_ref), (wa_ref, wb_ref, wc_ref, wo_ref), KW_WIDE)

    @pl.when(main)
    def _():
        is_sample = pl.program_id(0) - KW_WIDE >= ROWS_P // TM_MERGE
        c = jnp.where(is_sample, cs_ref[...], cp_ref[...])
        m = g0_ref[...].astype(F32) * _dot(a_ref[...], wa_ref[...])
        m = m + g1_ref[...].astype(F32) * _dot(b_ref[...], wb_ref[...])
        m = m + g2_ref[...].astype(F32) * _dot(c, wc_ref[...])
        h1 = h_ref[...] + _dot(m.astype(BF16), wo_ref[...])
        h1_ref[...] = h1
        hn_ref[...] = _rms(h1, gf_ref[...]).astype(BF16)


def _merge_call(a, b, c_p, c_s, gates, h, w_a, w_b, w_c, w_o, g_ffn, l):
    tm, kw = TM_MERGE, KW_WIDE
    np_ = ROWS_P // tm
    rows = functools.partial(_rows, tm=tm, lag=kw)
    return _call(
        _merge_body, (kw + ROWS // tm,),
        [rows(CONV_DIM), rows(GMLP_DIM),
         pl.BlockSpec((tm, ATTN_DIM), lambda s: (jnp.clip(s - kw, 0, np_ - 1), 0)),
         pl.BlockSpec((tm, ATTN_DIM), lambda s: (jnp.maximum(s - kw - np_, 0), 0)),
         rows(D_MODEL, cb=0), rows(D_MODEL, cb=1), rows(D_MODEL, cb=2), rows(D_MODEL),
         _wchunk(l, CONV_DIM, D_MODEL, kw), _wchunk(l, GMLP_DIM, D_MODEL, kw),
         _wchunk(l, ATTN_DIM, D_MODEL, kw), _wchunk(l, D_MODEL, D_MODEL, kw), _vec(l, D_MODEL)],
        [rows(D_MODEL), rows(D_MODEL)],
        [jax.ShapeDtypeStruct((ROWS, D_MODEL), F32), jax.ShapeDtypeStruct((ROWS, D_MODEL), BF16)],
        scratch=[_wcopy(CONV_DIM, D_MODEL), _wcopy(GMLP_DIM, D_MODEL), _wcopy(ATTN_DIM, D_MODEL),
                 _wcopy(D_MODEL, D_MODEL)],
        name="merge_out", vmem_mb=58)(a, b, c_p, c_s, gates, gates, gates, h, w_a, w_b, w_c, w_o, g_ffn)


def _ffn_body(hn_ref, w1_ref, w2_ref, h1_ref, o_ref):
    def delta():
        a = jnp.square(jnp.maximum(_dot(hn_ref[...], w1_ref[...]), 0.0)).astype(BF16)
        return _dot(a, w2_ref[...])

    @pl.when(pl.program_id(1) == 0)
    def _():
        o_ref[...] = h1_ref[...] + delta()

    @pl.when(pl.program_id(1) != 0)
    def _():
        o_ref[...] += delta()


def _ffn_call(hn, h1, w1, w2):
    return _call(
        _ffn_body, (NT, D_FF // TF),
        [pl.BlockSpec((TM, D_MODEL), lambda i, f: (i, 0)),
         pl.BlockSpec((D_MODEL, TF), lambda i, f: (0, f)),
         pl.BlockSpec((TF, D_MODEL), lambda i, f: (f, 0)),
         pl.BlockSpec((TM, D_MODEL), lambda i, f: (i, 0))],
        pl.BlockSpec((TM, D_MODEL), lambda i, f: (i, 0)),
        jax.ShapeDtypeStruct((ROWS, D_MODEL), F32),
        name="ffn", vmem_mb=62)(hn, w1, w2, h1)


def _ple_update(tile, h2_ref, pp_ref, ps_ref, wg_ref, wp_ref, gp_ref):
    h2 = h2_ref[...]
    p = jnp.where(tile == NT_P, ps_ref[...], pp_ref[...])
    gate = jax.nn.sigmoid(_dot(_rms(h2, gp_ref[...]).astype(BF16), wg_ref[...]))
    return h2 + gate * _dot(p.astype(BF16), wp_ref[...])


def _ple_body(h2_ref, pp_ref, ps_ref, wg_ref, wp_ref, gp_ref, gn_ref, h3_ref, xn_ref):
    h3 = _ple_update(pl.program_id(0), h2_ref, pp_ref, ps_ref, wg_ref, wp_ref, gp_ref)
    h3_ref[...] = h3
    xn_ref[...] = _rms(h3, gn_ref[...]).astype(BF16)


def _ple_last_body(h2_ref, pp_ref, ps_ref, wg_ref, wp_ref, gp_ref, gn_ref, yp_ref, ys_ref):
    tile = pl.program_id(0)

    def final():
        return _rms(_ple_update(tile, h2_ref, pp_ref, ps_ref, wg_ref, wp_ref, gp_ref), gn_ref[...])

    @pl.when(tile < NT_P)
    def _():
        yp_ref[...] = final()

    @pl.when(tile >= NT_P)
    def _():
        ys_ref[...] = final()


def _ple_call(h2, p_p, p_s, w_g, w_p, g_ple, g_next, l, last):
    assert ROWS_S == TM
    ins = [_rows(D_MODEL),
           pl.BlockSpec((None, TM, PLE_DIM), lambda s: (l, jnp.minimum(s, NT_P - 1), 0)),
           pl.BlockSpec((None, TM, PLE_DIM), lambda s: (l, 0, 0)),
           _resident((D_MODEL, D_MODEL), (0, 0)), _resident((PLE_DIM, D_MODEL), (0, 0)),
           _vec(l, D_MODEL), _vec(l + 1, D_MODEL)]
    args = (h2, p_p, p_s, w_g, w_p, g_ple, g_next)
    if not last:
        return _call(
            _ple_body, (NT,), ins, [_rows(D_MODEL)] * 2,
            [jax.ShapeDtypeStruct((ROWS, D_MODEL), F32), jax.ShapeDtypeStruct((ROWS, D_MODEL), BF16)],
            name="ple", vmem_mb=56)(*args)
    return _call(
        _ple_last_body, (NT,), ins,
        [pl.BlockSpec((TM, D_MODEL), lambda s: (jnp.minimum(s, NT_P - 1), 0)),
         pl.BlockSpec((TM, D_MODEL), lambda s: (0, 0))],
        [jax.ShapeDtypeStruct((ROWS_P, D_MODEL), F32), jax.ShapeDtypeStruct((ROWS_S, D_MODEL), F32)],
        name="ple_final", vmem_mb=56)(*args)


def _conv_rows(ext_ref, y_ref, cw_ref, ext_base, y_base, rc):
    for lb in range(CONV_DIM // CONV_LB):
        ls = slice(lb * CONV_LB, (lb + 1) * CONV_LB)
        acc = None
        for r in range(SUBLANES):
            rows = rc + (SUBLANES if r else 0)
            u = None
            for j in range(CONV_WIDTH):
                if (2 + j) % SUBLANES == r:
                    start = ext_base + (2 + j - r)
                    if not isinstance(start, int):
                        start = pl.multiple_of(start, SUBLANES)
                    term = cw_ref[j:j + 1, ls] * ext_ref[pl.ds(start, rows), ls]
                    u = term if u is None else u + term
            part = u[r:r + rc, :]
            acc = part if acc is None else acc + part
        y_ref[pl.ds(y_base, rc), ls] = acc


def _conv_body(a_ref, cache_ref, cw_ref, cb_ref, g_ref, b_ref, f1_ref, f2_ref,
               o_ref, f1b_ref, f2b_ref, ext_ref, y_ref):
    i = pl.program_id(0)
    hist = 32

    f1b_ref[...] = f1_ref[...].astype(BF16)
    f2b_ref[...] = f2_ref[...].astype(BF16)

    @pl.when(i < NT_P)
    def _():
        @pl.when(i % TILES_PER_SEQ == 0)
        def _():
            ext_ref[0:hist, :] = jnp.zeros((hist, CONV_DIM), F32)

        @pl.when(i % TILES_PER_SEQ != 0)
        def _():
            ext_ref[0:hist, :] = ext_ref[TM:TM + hist, :]

        ext_ref[hist:hist + TM, :] = a_ref[...]

        def chunk(c, carry):
            r0 = pl.multiple_of(c * CONV_RC, CONV_RC)
            _conv_rows(ext_ref, y_ref, cw_ref, r0, r0, CONV_RC)
            return carry

        lax.fori_loop(0, TM // CONV_RC, chunk, 0)

    @pl.when(i == NT_P)
    def _():
        def seq(s, carry):
            r0 = pl.multiple_of(s * DEC_SEQ, DEC_SEQ)
            ext_ref[hist - CONV_HIST:hist, :] = cache_ref[s]
            ext_ref[hist:hist + DEC_SEQ, :] = a_ref[pl.ds(r0, DEC_SEQ), :]
            _conv_rows(ext_ref, y_ref, cw_ref, 0, r0, DEC_SEQ)
            return carry

        lax.fori_loop(0, DEC_BATCH, seq, 0)

    y = _ln(y_ref[...] + cb_ref[...], g_ref[...], b_ref[...])
    o_ref[...] = (y * jax.nn.sigmoid(y)).astype(BF16)


FF_SLABS = NT - 1


def _conv_call(a, cache_conv, conv_w, conv_b, ln_g, ln_b, w_ff1, w_ff2, l):
    r1, r2 = D_MODEL // FF_SLABS, D_FF // FF_SLABS
    slab = lambda i: jnp.minimum(i, FF_SLABS - 1)
    return _call(
        _conv_body, (NT,),
        [_rows(CONV_DIM),
         _resident((None, DEC_BATCH, CONV_HIST, CONV_DIM), (l, 0, 0, 0)),
         _resident((None, CONV_WIDTH, CONV_DIM), (l, 0, 0)),
         _vec(l, CONV_DIM), _vec(l, CONV_DIM), _vec(l, CONV_DIM),
         pl.BlockSpec((None, r1, D_FF), lambda i: (l, slab(i), 0)),
         pl.BlockSpec((None, r2, D_MODEL), lambda i: (l, slab(i), 0))],
        [_rows(CONV_DIM), pl.BlockSpec((r1, D_FF), lambda i: (slab(i), 0)),
         pl.BlockSpec((r2, D_MODEL), lambda i: (slab(i), 0))],
        [jax.ShapeDtypeStruct((ROWS, CONV_DIM), BF16), jax.ShapeDtypeStruct((D_MODEL, D_FF), BF16),
         jax.ShapeDtypeStruct((D_FF, D_MODEL), BF16)],
        scratch=[pltpu.VMEM((TM + 32, CONV_DIM), F32), pltpu.VMEM((TM, CONV_DIM), F32)],
        name="conv_mixer", vmem_mb=48)(a, cache_conv, conv_w, conv_b, ln_g, ln_b, w_ff1, w_ff2)


def _toeplitz_rows(tab_ref, h, rows):
    t = jnp.broadcast_to(tab_ref[h:h + 1, :], (rows, TOEP))
    return pltpu.roll(t, TOEP - QB + 1, 1, stride=1, stride_axis=0)


def _attn_p_body(q_ref, *refs):
    nkb = QPS + 2
    k_refs, v_refs = refs[:nkb], refs[nkb:2 * nkb]
    tab_ref, cg_ref, cp_ref, o_ref, wgb_ref, wpb_ref, bias_ref = refs[2 * nkb:]
    m = pl.program_id(1)

    wgb_ref[...] = cg_ref[...].astype(BF16)
    wpb_ref[...] = cp_ref[...].astype(BF16)

    @pl.when((pl.program_id(0) == 0) & (m == 0))
    def _():
        r = lax.broadcasted_iota(jnp.int32, (QB, KB), 0) // CHUNK
        j = lax.broadcasted_iota(jnp.int32, (QB, KB), 1) // CHUNK
        band = (j >= r) & (j <= r + LEFT_CHUNKS)
        for h in range(N_HEADS):
            bias = jnp.where(band, _toeplitz_rows(tab_ref, h, QB)[:, :KB], NEG)
            for s in range(3):
                bias_ref[h, s] = bias[:, s * QB:(s + 1) * QB]
            bias_ref[h, 3] = jnp.full((QB, QB), NEG, F32)

    for j in range(QPS):
        rs = slice(j * QB, (j + 1) * QB)
        slot = [jnp.where(QPS * m + j + s - 2 >= 0, s, 3) for s in range(3)]
        for h in range(N_HEADS):
            hs = slice(h * HEAD_DIM, (h + 1) * HEAD_DIM)
            qh = q_ref[rs, hs]
            sc = [_dot_nt(qh, k_refs[j + s][:, hs]) + bias_ref[h, slot[s]] for s in range(3)]
            mx = jnp.max(functools.reduce(jnp.maximum, sc), axis=-1, keepdims=True)
            e = [jnp.exp2(x - mx) for x in sc]
            den = jnp.sum(functools.reduce(jnp.add, e), axis=-1, keepdims=True)
            o = functools.reduce(jnp.add, [_dot(e[s].astype(BF16), v_refs[j + s][:, hs]) for s in range(3)])
            o_ref[rs, hs] = (o * (1.0 / den)).astype(BF16)


def _attn_p_call(q, k, v, toep, w_g, w_p, l):
    nq = SEQ // QB
    steps = nq // QPS
    nkb = QPS + 2
    slabs = BATCH * steps
    rg, rp = D_MODEL // slabs, PLE_DIM // slabs
    slab = lambda b, m: b * steps + m

    def kv_spec(t):
        return pl.BlockSpec((QB, ATTN_DIM), lambda b, m: (b * nq + jnp.maximum(QPS * m - 2 + t, 0), 0))

    q_spec = pl.BlockSpec((QPS * QB, ATTN_DIM), lambda b, m: (b * steps + m, 0))
    return _call(
        _attn_p_body, (BATCH, steps),
        [q_spec] + [kv_spec(t) for t in range(nkb)] * 2 + [_resident((None, N_HEADS, TOEP), (l, 0, 0)),
         pl.BlockSpec((None, rg, D_MODEL), lambda b, m: (l, slab(b, m), 0)),
         pl.BlockSpec((None, rp, D_MODEL), lambda b, m: (l, slab(b, m), 0))],
        [q_spec, pl.BlockSpec((rg, D_MODEL), lambda b, m: (slab(b, m), 0)),
         pl.BlockSpec((rp, D_MODEL), lambda b, m: (slab(b, m), 0))],
        [jax.ShapeDtypeStruct((ROWS_P, ATTN_DIM), BF16), jax.ShapeDtypeStruct((D_MODEL, D_MODEL), BF16),
         jax.ShapeDtypeStruct((PLE_DIM, D_MODEL), BF16)],
        scratch=[pltpu.VMEM((N_HEADS, 4, QB, QB), F32)],
        name="attn_prompt", vmem_mb=48)(q, *([k] * nkb), *([v] * nkb), toep, w_g, w_p)


def _attn_s_body(q_ref, kn_ref, vn_ref, kc_ref, vc_ref, tab_ref, o_ref, bias_ref):
    w = ATTN_REACH

    @pl.when(pl.program_id(0) == 0)
    def _():
        for h in range(N_HEADS):
            bias_ref[h] = _toeplitz_rows(tab_ref, h, DEC_SEQ)[:, :w + DEC_SEQ]

    for b in range(SEQS_PER_STEP):
        rs = slice(b * DEC_SEQ, (b + 1) * DEC_SEQ)
        for h in range(N_HEADS):
            hs = slice(h * HEAD_DIM, (h + 1) * HEAD_DIM)
            rows_h = pl.ds(h, w, stride=N_HEADS)
            qh = q_ref[rs, hs]
            s_c = _dot_nt(qh, kc_ref[b, rows_h, :].astype(BF16)) + bias_ref[h, :, :w]
            s_n = _dot_nt(qh, kn_ref[rs, hs]) + bias_ref[h, :, w:]
            mx = jnp.maximum(jnp.max(s_c, axis=-1, keepdims=True), jnp.max(s_n, axis=-1, keepdims=True))
            e_c = jnp.exp2(s_c - mx)
            e_n = jnp.exp2(s_n - mx)
            den = jnp.sum(e_c, axis=-1, keepdims=True) + jnp.sum(e_n, axis=-1, keepdims=True)
            o = (_dot(e_c.astype(BF16), vc_ref[b, rows_h, :].astype(BF16))
                 + _dot(e_n.astype(BF16), vn_ref[rs, hs]))
            o_ref[rs, hs] = (o * (1.0 / den)).astype(BF16)


def _attn_s_call(q, k, v, cache_k, cache_v, toep, l):
    nb, rows = SEQS_PER_STEP, SEQS_PER_STEP * DEC_SEQ
    first = ROWS_P // rows
    new = pl.BlockSpec((rows, ATTN_DIM), lambda i: (first + i, 0))
    cache = pl.BlockSpec((None, nb, ATTN_REACH * N_HEADS, HEAD_DIM), lambda i: (l, i, 0, 0))
    return _call(
        _attn_s_body, (DEC_BATCH // nb,),
        [new, new, new, cache, cache, _resident((None, N_HEADS, TOEP), (l, 0, 0))],
        pl.BlockSpec((rows, ATTN_DIM), lambda i: (i, 0)),
        jax.ShapeDtypeStruct((ROWS_S, ATTN_DIM), BF16),
        scratch=[pltpu.VMEM((N_HEADS, DEC_SEQ, ATTN_REACH + DEC_SEQ), F32)],
        name="attn_sample", vmem_mb=40)(q, k, v, cache_k, cache_v, toep)


def _toeplitz_table(rel_tab):
    left = 3 * QB - 1 - REL_CLIP
    right = TOEP - left - (2 * REL_CLIP + 1)
    pad = [(0, 0)] * (rel_tab.ndim - 1) + [(left, right)]
    return jnp.pad(rel_tab[..., ::-1] * LOG2E, pad, mode="edge").astype(F32)


def kernel(x_prompt, x_sample, p_prompt, p_sample, cache_conv, cache_k, cache_v, norm_mix, w_in, conv_w, conv_b, conv_ln_g, conv_ln_b, w_a_out, gmlp_ln_g, gmlp_ln_b, gmlp_ws, gmlp_bs, w_b_out, attn_rel_bias, w_c_out, w_o, norm_ffn, w_ff1, w_ff2, norm_ple, w_ple_gate, w_ple_proj, norm_final):
    p_p = p_prompt.reshape(DEPTH, ROWS_P, PLE_DIM)
    p_s = p_sample.reshape(DEPTH, ROWS_S, PLE_DIM)
    cache_k = cache_k.reshape(DEPTH, DEC_BATCH, ATTN_REACH * N_HEADS, HEAD_DIM)
    cache_v = cache_v.reshape(DEPTH, DEC_BATCH, ATTN_REACH * N_HEADS, HEAD_DIM)
    toep = _toeplitz_table(attn_rel_bias)

    vec = lambda x: x.reshape(x.shape[0], 1, x.shape[-1])
    g_mix = vec(jnp.concatenate([norm_mix, norm_final[None]], axis=0))
    conv_b3, conv_g3, conv_lb3 = vec(conv_b), vec(conv_ln_g), vec(conv_ln_b)
    gm_g3, gm_b3 = vec(gmlp_ln_g), vec(gmlp_ln_b)
    g_ffn3, g_ple3 = vec(norm_ffn), vec(norm_ple)

    outs = [[] for _ in range(7)]
    h, xn = _first_call(x_prompt.reshape(ROWS_P, D_MODEL), x_sample.reshape(ROWS_S, D_MODEL), g_mix)
    for l in range(DEPTH):
        q, k, v, k_rows, v_rows = _qkv_call(xn, w_in, l)
        gates = _gate_call(xn, w_in, l)

        ws = gmlp_ws[l]
        corner = jnp.tile(ws[:, :DEC_SEQ, :DEC_SEQ], (1, GMLP_CHUNK // DEC_SEQ, GMLP_CHUNK // DEC_SEQ))
        bias_p = jnp.repeat(gmlp_bs[l].T, GMLP_GC, axis=1)
        bias_s = jnp.tile(bias_p[:DEC_SEQ], (GMLP_CHUNK // DEC_SEQ, 1))
        a, yb_in, v_s = _glu_gmlp_call(xn, w_in, gm_g3, gm_b3, jnp.stack([ws, corner]),
                                       jnp.stack([bias_p, bias_s]), l)
        ya_in, w_ff1_b, w_ff2_b = _conv_call(a, cache_conv, conv_w, conv_b3, conv_g3, conv_lb3, w_ff1, w_ff2, l)

        o_p, w_pg_b, w_pp_b = _attn_p_call(q, k, v, toep, w_ple_gate, w_ple_proj, l)
        o_s = _attn_s_call(q, k, v, cache_k, cache_v, toep, l)

        h1, hn = _merge_call(ya_in, yb_in, o_p, o_s, gates, h, w_a_out, w_b_out, w_c_out, w_o, g_ffn3, l)
        h2 = _ffn_call(hn, h1, w_ff1_b, w_ff2_b)
        last = l == DEPTH - 1
        h, xn = _ple_call(h2, p_p, p_s, w_pg_b, w_pp_b, g_ple3, g_mix, l, last)

        outs[0].append(jnp.stack([a[(b + 1) * SEQ - CONV_HIST:(b + 1) * SEQ] for b in range(BATCH)]))
        outs[1].append(a[ROWS_P:].reshape(DEC_BATCH, DEC_SEQ, CONV_DIM)[:, DEC_SEQ - CONV_HIST:])
        for j, rows in enumerate((k_rows, v_rows)):
            rows = rows.reshape(KV_SLOTS, TM, N_HEADS, HEAD_DIM)
            outs[2 + j].append(rows[:BATCH])
            outs[4 + j].append(rows[BATCH].reshape(DEC_BATCH, DEC_SEQ, N_HEADS, HEAD_DIM))
        outs[6].append(v_s.reshape(DEC_BATCH, DEC_SEQ, GMLP_DIM))

    y_prompt = h.reshape(BATCH, SEQ, D_MODEL)
    y_sample = xn.reshape(DEC_BATCH, DEC_SEQ, D_MODEL)
    return (y_prompt, y_sample) + tuple(jnp.stack(x) for x in outs)
```
